```python
import jax, jax.numpy as jnp
from jax import lax
import numpy as np

D_MODEL = 4096
BATCH = 4
SEQ = 2048
DEPTH = 4

N_MIXERS = 2
N_MOBA_LAYERS = (DEPTH + 1) // 2
N_MLSTM_LAYERS = DEPTH // 2

MOBA_HEADS = 32
MOBA_HEAD_DIM = D_MODEL // MOBA_HEADS
MOBA_BLOCK = 256
MOBA_TOPK = 3
MOBA_QCHUNK = 16

MLSTM_HEADS = 8
MLSTM_QK_DIM = D_MODEL // 2 // MLSTM_HEADS
MLSTM_V_DIM = D_MODEL // MLSTM_HEADS
MLSTM_CHUNK = 64
MLSTM_GATE_CAP = 15.0
MLSTM_IN_COLS = 2 * MLSTM_HEADS * MLSTM_QK_DIM + 2 * MLSTM_HEADS * MLSTM_V_DIM + 2 * MLSTM_HEADS

N_EXPERTS = 32
MOE_TOPK = 4
EXPERT_FF = 384
SWIGLU_LIMIT = 7.0
SWIGLU_ALPHA = 1.702

DN_ALPHA = (2 * DEPTH) ** 0.25
DN_BETA = (8 * DEPTH) ** -0.25
LN_EPS = 1e-5
RMS_EPS = 1e-6

kernel_name = "moba_mlstm_moe_deepnorm_hybrid"


def layer_norm(x, g, b):
    xf = x.astype(jnp.float32)
    mu = jnp.mean(xf, axis=-1, keepdims=True)
    var = jnp.mean(jnp.square(xf - mu), axis=-1, keepdims=True)
    return ((xf - mu) * lax.rsqrt(var + LN_EPS) * g + b).astype(x.dtype)


def moba_mixer(x, w_in, w_out):
    B, T, _ = x.shape
    H, HD, BS, QC = MOBA_HEADS, MOBA_HEAD_DIM, MOBA_BLOCK, MOBA_QCHUNK
    tp = -(-T // BS) * BS
    nb, nq = tp // BS, tp // QC
    k_sel = min(MOBA_TOPK, nb)
    qkv = jnp.pad(x @ w_in, ((0, 0), (0, tp - T), (0, 0)))
    qkv = qkv.reshape(B, tp, 3, H, HD).transpose(2, 0, 3, 1, 4)
    q, k, v = qkv[0], qkv[1], qkv[2]
    kb = k.reshape(B, H, nb, BS, HD)
    vb = v.reshape(B, H, nb, BS, HD)
    k_mean = jnp.mean(kb, axis=3)
    gate = jnp.einsum('bhtd,bhnd->bhtn', q, k_mean).astype(jnp.float32)
    q_blk = jnp.arange(tp) // BS
    fully_past = jnp.arange(nb)[None, :] < q_blk[:, None]
    gate = jnp.where(fully_past, gate, -jnp.inf)
    _, sel = lax.top_k(gate, k_sel)
    q_chunks = q.reshape(B, H, nq, QC, HD).transpose(2, 0, 1, 3, 4)
    sel_chunks = sel.reshape(B, H, nq, QC, k_sel).transpose(2, 0, 1, 3, 4)
    bi = jnp.arange(B)[:, None, None, None]
    hi = jnp.arange(H)[None, :, None, None]
    scale = HD ** -0.5

    def attend(args):
        c, q_c, sel_c = args
        start = c * QC
        j = start // BS
        k_g = kb[bi, hi, sel_c]
        v_g = vb[bi, hi, sel_c]
        k_own = lax.dynamic_index_in_dim(kb, j, axis=2, keepdims=False)
        v_own = lax.dynamic_index_in_dim(vb, j, axis=2, keepdims=False)
        s_sel = jnp.einsum('bhqd,bhqnkd->bhqnk', q_c, k_g).astype(jnp.float32) * scale
        valid = (jnp.arange(k_sel) < j)[:, None]
        s_sel = jnp.where(valid, s_sel, -jnp.inf)
        s_own = jnp.einsum('bhqd,bhkd->bhqk', q_c, k_own).astype(jnp.float32) * scale
        q_pos = start + jnp.arange(QC)
        k_pos = j * BS + jnp.arange(BS)
        s_own = jnp.where(k_pos[None, :] <= q_pos[:, None], s_own, -jnp.inf)
        s = jnp.concatenate([s_sel.reshape(B, H, QC, k_sel * BS), s_own], axis=-1)
        p = jax.nn.softmax(s, axis=-1).astype(v.dtype)
        p_sel = p[..., :k_sel * BS].reshape(B, H, QC, k_sel, BS)
        return (jnp.einsum('bhqnk,bhqnkd->bhqd', p_sel, v_g)
                + jnp.einsum('bhqk,bhkd->bhqd', p[..., k_sel * BS:], v_own))

    o = lax.map(attend, (jnp.arange(nq), q_chunks, sel_chunks))
    o = o.transpose(1, 0, 3, 2, 4).reshape(B, tp, H * HD)[:, :T]
    return o @ w_out


def mlstm_mixer(x, w_in, b_gates, norm_g, w_out):
    B, T, _ = x.shape
    H, DK, DV, L = MLSTM_HEADS, MLSTM_QK_DIM, MLSTM_V_DIM, MLSTM_CHUNK
    nc = T // L
    proj = x @ w_in
    q, k, v, o_pre, gates = jnp.split(
        proj, [H * DK, 2 * H * DK, 2 * H * DK + H * DV, 2 * H * DK + 2 * H * DV], axis=-1)

    def heads(t, d):
        return t.reshape(B, nc, L, H, d).transpose(1, 0, 3, 2, 4)

    q = heads(q, DK) * (DK ** -0.5)
    k = heads(k, DK)
    v = heads(v, DV)
    gates = gates.astype(jnp.float32) + b_gates
    gates = MLSTM_GATE_CAP * jnp.tanh(gates / MLSTM_GATE_CAP)

    def gheads(t):
        return t.reshape(B, nc, L, H).transpose(1, 0, 3, 2)

    log_i = gheads(gates[..., :H])
    log_f = gheads(jax.nn.log_sigmoid(gates[..., H:]))
    b = jnp.cumsum(log_f, axis=-1)
    g = b[..., -1]
    causal = jnp.tril(jnp.ones((L, L), dtype=bool))
    d_intra = jnp.where(causal, b[..., :, None] - b[..., None, :] + log_i[..., None, :], -jnp.inf)
    w_src = g[..., None] - b + log_i

    def step(carry, inp):
        C, n, m = carry
        q_c, k_c, v_c, b_c, g_c, d_c, ws_c = inp
        qf, kf, vf = q_c.astype(jnp.float32), k_c.astype(jnp.float32), v_c.astype(jnp.float32)
        m_inter = b_c + m[..., None]
        m_t = jnp.maximum(m_inter, jnp.max(d_c, axis=-1))
        w_inter = jnp.exp(m_inter - m_t)
        a = jnp.exp(d_c - m_t[..., None]) * jnp.einsum('bhtd,bhsd->bhts', qf, kf)
        num = w_inter[..., None] * jnp.einsum('bhtd,bhde->bhte', qf, C) + jnp.einsum('bhts,bhse->bhte', a, vf)
        den = w_inter * jnp.einsum('bhtd,bhd->bht', qf, n) + jnp.sum(a, axis=-1)
        h = num / jnp.maximum(jnp.abs(den), jnp.exp(-m_t))[..., None]
        m_new = jnp.maximum(g_c + m, jnp.max(ws_c, axis=-1))
        decay = jnp.exp(g_c + m - m_new)
        ws = jnp.exp(ws_c - m_new[..., None])
        C_new = decay[..., None, None] * C + jnp.einsum('bhsd,bhse->bhde', kf * ws[..., None], vf)
        n_new = decay[..., None] * n + jnp.einsum('bhsd,bhs->bhd', kf, ws)
        return (C_new, n_new, m_new), h

    init = (jnp.zeros((B, H, DK, DV), jnp.float32), jnp.zeros((B, H, DK), jnp.float32),
            jnp.zeros((B, H), jnp.float32))
    _, h = lax.scan(step, init, (q, k, v, b, g, d_intra, w_src))
    h = h.transpose(1, 0, 3, 2, 4).reshape(B, T, H, DV)
    h = h * lax.rsqrt(jnp.mean(jnp.square(h), axis=-1, keepdims=True) + RMS_EPS)
    h = h.reshape(B, T, H * DV) * norm_g
    out = (jax.nn.sigmoid(o_pre.astype(jnp.float32)) * h).astype(x.dtype)
    return out @ w_out


def moe_ffn(x, router_w, router_b, w_gate_up, b_gate_up, w_down, b_down):
    B, T, D = x.shape
    F = EXPERT_FF
    h = x.reshape(B * T, D)
    logits = (h @ router_w).astype(jnp.float32) + router_b
    top_val, top_idx = lax.top_k(logits, MOE_TOPK)
    top_p = jax.nn.softmax(top_val, axis=-1)
    combine = jnp.einsum('nk,nke->ne', top_p, jax.nn.one_hot(top_idx, N_EXPERTS, dtype=jnp.float32))
    gu = jnp.einsum('nd,edf->enf', h, w_gate_up) + b_gate_up[:, None, :]
    gate = jnp.minimum(gu[..., :F], SWIGLU_LIMIT)
    up = jnp.clip(gu[..., F:], -SWIGLU_LIMIT, SWIGLU_LIMIT)
    act = (up + 1.0) * gate * jax.nn.sigmoid(SWIGLU_ALPHA * gate) * combine.T[:, :, None].astype(h.dtype)
    y = jnp.einsum('enf,efd->nd', act, w_down) + combine.astype(h.dtype) @ b_down
    return y.reshape(B, T, D)


def setup_inputs(seed: int = 0) -> dict:
    key = jax.random.key(seed)
    ks = jax.random.split(key, 16)
    D, E, F, H = D_MODEL, N_EXPERTS, EXPERT_FF, MLSTM_HEADS

    def nrm(k, shape, scale):
        return jax.random.normal(k, shape, jnp.float32) * scale

    x = nrm(ks[0], (BATCH, SEQ, D), 1.0)
    moba_cols = jnp.concatenate([jnp.ones((2 * D,), jnp.float32), jnp.full((D,), DN_BETA, jnp.float32)])
    moba_w_in = nrm(ks[1], (N_MOBA_LAYERS, D, 3 * D), D ** -0.5) * moba_cols
    moba_w_out = nrm(ks[2], (N_MOBA_LAYERS, D, D), D ** -0.5 * DN_BETA)
    mlstm_cols = jnp.concatenate([
        jnp.ones((2 * H * MLSTM_QK_DIM,), jnp.float32),
        jnp.full((H * MLSTM_V_DIM,), DN_BETA, jnp.float32),
        jnp.ones((H * MLSTM_V_DIM + 2 * H,), jnp.float32)])
    mlstm_w_in = nrm(ks[3], (N_MLSTM_LAYERS, D, MLSTM_IN_COLS), D ** -0.5) * mlstm_cols
    i_bias = nrm(ks[4], (N_MLSTM_LAYERS, H), 0.1)
    f_bias = jnp.linspace(3.0, 6.0, H, dtype=jnp.float32)[None, :] + nrm(ks[5], (N_MLSTM_LAYERS, H), 0.1)
    mlstm_b_gates = jnp.concatenate([i_bias, f_bias], axis=-1)
    mlstm_norm_g = 1.0 + nrm(ks[6], (N_MLSTM_LAYERS, H * MLSTM_V_DIM), 0.02)
    mlstm_w_out = nrm(ks[7], (N_MLSTM_LAYERS, D, D), D ** -0.5 * DN_BETA)
    ln_g = 1.0 + nrm(ks[8], (DEPTH, 2, D), 0.02)
    ln_b = nrm(ks[9], (DEPTH, 2, D), 0.02)
    router_w = nrm(ks[10], (DEPTH, D, E), D ** -0.5)
    router_b = nrm(ks[11], (DEPTH, E), 0.01)
    moe_w_gate_up = nrm(ks[12], (DEPTH, E, D, 2 * F), D ** -0.5)
    moe_b_gate_up = nrm(ks[13], (DEPTH, E, 2 * F), 0.01)
    moe_w_down = nrm(ks[14], (DEPTH, E, F, D), F ** -0.5 * DN_BETA)
    moe_b_down = nrm(ks[15], (DEPTH, E, D), 0.01)
    return {"x": x, "moba_w_in": moba_w_in, "moba_w_out": moba_w_out,
            "mlstm_w_in": mlstm_w_in, "mlstm_b_gates": mlstm_b_gates, "mlstm_norm_g": mlstm_norm_g,
            "mlstm_w_out": mlstm_w_out, "ln_g": ln_g, "ln_b": ln_b,
            "router_w": router_w, "router_b": router_b,
            "moe_w_gate_up": moe_w_gate_up, "moe_b_gate_up": moe_b_gate_up,
            "moe_w_down": moe_w_down, "moe_b_down": moe_b_down}


def reference(x, moba_w_in, moba_w_out, mlstm_w_in, mlstm_b_gates, mlstm_norm_g, mlstm_w_out,
              ln_g, ln_b, router_w, router_b, moe_w_gate_up, moe_b_gate_up, moe_w_down, moe_b_down):
    for i in range(DEPTH):
        slot = i // N_MIXERS
        if i % N_MIXERS == 0:
            y = moba_mixer(x, moba_w_in[slot], moba_w_out[slot])
        else:
            y = mlstm_mixer(x, mlstm_w_in[slot], mlstm_b_gates[slot], mlstm_norm_g[slot], mlstm_w_out[slot])
        x = layer_norm(DN_ALPHA * x + y, ln_g[i, 0], ln_b[i, 0])
        y = moe_ffn(x, router_w[i], router_b[i], moe_w_gate_up[i], moe_b_gate_up[i],
                    moe_w_down[i], moe_b_down[i])
        x = layer_norm(DN_ALPHA * x + y, ln_g[i, 1], ln_b[i, 1])
    return x
```

```python
import functools

import jax
import jax.numpy as jnp
from jax import lax
from jax.experimental import pallas as pl
from jax.experimental.pallas import tpu as pltpu

F32 = jnp.float32
BF16 = jnp.bfloat16

D_MODEL = 4096
DEPTH = 4
MOBA_HEADS = 32
MOBA_HEAD_DIM = D_MODEL // MOBA_HEADS
MOBA_BLOCK = 256
MOBA_TOPK = 3
MLSTM_HEADS = 8
MLSTM_QK_DIM = D_MODEL // 2 // MLSTM_HEADS
MLSTM_V_DIM = D_MODEL // MLSTM_HEADS
MLSTM_GATE_CAP = 15.0
N_EXPERTS = 32
MOE_TOPK = 4
EXPERT_FF = 384
SWIGLU_LIMIT = 7.0
SWIGLU_ALPHA = 1.702
DN_ALPHA = (2 * DEPTH) ** 0.25
LN_EPS = 1e-5
RMS_EPS = 1e-6

LANES = 128
SUBLANES = 8
VMEM_LIMIT_BYTES = 56 * 1024 * 1024

MLSTM_KERNEL_CHUNK = 256
MOE_ROW_TILE = 256
COMBINE_ROW_TILE = 128


def _params(semantics):
    return pltpu.CompilerParams(dimension_semantics=semantics, vmem_limit_bytes=VMEM_LIMIT_BYTES)


def _pick(n, target):
    t = min(n, target)
    while n % t:
        t //= 2
    return t


def _matmul_kernel(a_ref, b_ref, o_ref, acc_ref):
    k = pl.program_id(2)

    @pl.when(k == 0)
    def _():
        acc_ref[...] = jnp.zeros_like(acc_ref)

    acc_ref[...] += jnp.dot(a_ref[...], b_ref[...].astype(BF16), preferred_element_type=F32)

    @pl.when(k == pl.num_programs(2) - 1)
    def _():
        o_ref[...] = acc_ref[...].astype(o_ref.dtype)


def _matmul(a, b, layer, out_dtype, n_cols=None, tm=1024, tn=1024, tk=512):
    M, K = a.shape
    N = b.shape[2] if n_cols is None else n_cols
    tm, tn, tk = _pick(M, tm), _pick(N, tn), _pick(K, tk)
    return pl.pallas_call(
        _matmul_kernel,
        grid=(M // tm, N // tn, K // tk),
        in_specs=[pl.BlockSpec((tm, tk), lambda i, j, k: (i, k)),
                  pl.BlockSpec((None, tk, tn), lambda i, j, k: (layer, k, j))],
        out_specs=pl.BlockSpec((tm, tn), lambda i, j, k: (i, j)),
        out_shape=jax.ShapeDtypeStruct((M, N), out_dtype),
        scratch_shapes=[pltpu.VMEM((tm, tn), F32)],
        compiler_params=_params(("parallel", "parallel", "arbitrary")),
        name="dense_matmul",
    )(a, b)


def _layer_norm_rows(z, g, b):
    mu = jnp.mean(z, axis=-1, keepdims=True)
    zc = z - mu
    var = jnp.mean(zc * zc, axis=-1, keepdims=True)
    return zc * lax.rsqrt(var + LN_EPS) * g + b


def _res_ln_kernel(x_ref, y_ref, g_ref, b_ref, of_ref, ob_ref):
    z = DN_ALPHA * x_ref[...] + y_ref[...].astype(F32)
    out = _layer_norm_rows(z, g_ref[...], b_ref[...])
    of_ref[...] = out
    ob_ref[...] = out.astype(BF16)


def _res_ln(x, y, g, b, tm=256):
    M, D = x.shape
    tm = _pick(M, tm)
    row = pl.BlockSpec((tm, D), lambda i: (i, 0))
    vec = pl.BlockSpec((1, D), lambda i: (0, 0))
    return pl.pallas_call(
        _res_ln_kernel,
        grid=(M // tm,),
        in_specs=[row, row, vec, vec],
        out_specs=[row, row],
        out_shape=[jax.ShapeDtypeStruct((M, D), F32), jax.ShapeDtypeStruct((M, D), BF16)],
        compiler_params=_params(("parallel",)),
        name="residual_layernorm",
    )(x, y, g.reshape(1, D), b.reshape(1, D))


def _moba_kernel(q_ref, k_ref, v_ref, o_ref, kmean_ref, m_ref, l_ref, acc_ref, *, nb, bs, topk, scale):
    j = pl.program_id(2)
    hd = q_ref.shape[-1]

    @pl.when(j == 0)
    def _():
        kmean_ref[...] = jnp.zeros_like(kmean_ref)
        for n in range(nb):
            kblk = k_ref[n * bs:(n + 1) * bs, :].astype(F32)
            kmean_ref[n:n + 1, :] = jnp.mean(kblk, axis=0, keepdims=True)

    q = q_ref[...]
    contract_last = (((1,), (1,)), ((), ()))

    kmean = kmean_ref[...]
    km_hi = kmean.astype(BF16)
    km_lo = (kmean - km_hi.astype(F32)).astype(BF16)
    gate = (lax.dot_general(q, km_hi, contract_last, preferred_element_type=F32)
            + lax.dot_general(q, km_lo, contract_last, preferred_element_type=F32))
    lane = lax.broadcasted_iota(jnp.int32, gate.shape, 1)
    gate = jnp.where(lane < j, gate, -jnp.inf)

    k_own = k_ref[pl.ds(pl.multiple_of(j * bs, bs), bs), :]
    v_own = v_ref[pl.ds(pl.multiple_of(j * bs, bs), bs), :]
    s = lax.dot_general(q, k_own, contract_last, preferred_element_type=F32) * scale
    row = lax.broadcasted_iota(jnp.int32, s.shape, 0)
    col = lax.broadcasted_iota(jnp.int32, s.shape, 1)
    s = jnp.where(col <= row, s, -jnp.inf)
    m0 = jnp.max(s, axis=1, keepdims=True)
    p = jnp.exp(s - m0)
    m_ref[...] = m0
    l_ref[...] = jnp.sum(p, axis=1, keepdims=True)
    acc_ref[...] = jnp.dot(p.astype(BF16), v_own, preferred_element_type=F32)

    for n in range(nb - 1):
        @pl.when(n < j)
        def _(n=n):
            gn = gate[:, n:n + 1]
            beats = (gate > gn) | ((gate == gn) & (lane < n))
            cnt = jnp.sum(beats.astype(F32), axis=1, keepdims=True)
            sel = cnt < float(topk)
            k_n = k_ref[n * bs:(n + 1) * bs, :]
            v_n = v_ref[n * bs:(n + 1) * bs, :]
            sn = lax.dot_general(q, k_n, contract_last, preferred_element_type=F32) * scale
            sn = jnp.where(sel, sn, -jnp.inf)
            m_old = m_ref[...]
            m_new = jnp.maximum(m_old, jnp.max(sn, axis=1, keepdims=True))
            a = jnp.exp(m_old - m_new)
            pn = jnp.exp(sn - m_new)
            l_ref[...] = a * l_ref[...] + jnp.sum(pn, axis=1, keepdims=True)
            acc_ref[...] = a * acc_ref[...] + jnp.dot(pn.astype(BF16), v_n, preferred_element_type=F32)
            m_ref[...] = m_new

    o_ref[...] = (acc_ref[...] / l_ref[...]).astype(o_ref.dtype)


def _moba_attention(qkv, batch, seq, heads, hd, bs, topk):
    assert seq % bs == 0 and hd % LANES == 0
    nb = seq // bs
    assert nb <= LANES
    kernel = functools.partial(_moba_kernel, nb=nb, bs=bs, topk=min(topk, nb), scale=hd ** -0.5)
    return pl.pallas_call(
        kernel,
        grid=(batch, heads, nb),
        in_specs=[pl.BlockSpec((bs, hd), lambda b, h, j: (b * nb + j, h)),
                  pl.BlockSpec((seq, hd), lambda b, h, j: (b, heads + h)),
                  pl.BlockSpec((seq, hd), lambda b, h, j: (b, 2 * heads + h))],
        out_specs=pl.BlockSpec((bs, hd), lambda b, h, j: (b * nb + j, h)),
        out_shape=jax.ShapeDtypeStruct((batch * seq, heads * hd), BF16),
        scratch_shapes=[pltpu.VMEM((LANES, hd), F32),
                        pltpu.VMEM((bs, 1), F32), pltpu.VMEM((bs, 1), F32), pltpu.VMEM((bs, hd), F32)],
        compiler_params=_params(("parallel", "parallel", "arbitrary")),
        name="moba_attention",
    )(qkv, qkv, qkv)


def _split3(x):
    h1 = x.astype(BF16)
    r1 = x - h1.astype(F32)
    h2 = r1.astype(BF16)
    r2 = r1 - h2.astype(F32)
    return h1, h2, r2.astype(BF16)


def _soft_cap(z):
    return MLSTM_GATE_CAP * jnp.tanh(z / MLSTM_GATE_CAP)


def _log_sigmoid(z):
    return jnp.minimum(z, 0.0) - jnp.log1p(jnp.exp(-jnp.abs(z)))


def _mlstm_kernel(q_ref, k_ref, v_ref, op_ref, gc_ref, gr_ref, bc_ref, br_ref, ng_ref, o_ref,
                  c_ref, n_ref, m_ref, *, heads):
    h = pl.program_id(1)
    c = pl.program_id(2)
    L, dk = q_ref.shape

    @pl.when(c == 0)
    def _():
        c_ref[...] = jnp.zeros_like(c_ref)
        n_ref[...] = jnp.zeros_like(n_ref)
        m_ref[...] = jnp.zeros_like(m_ref)

    gcol = _soft_cap(gc_ref[...] + bc_ref[...])
    lane = lax.broadcasted_iota(jnp.int32, gcol.shape, 1)
    li_col = jnp.sum(jnp.where(lane == h, gcol, 0.0), axis=1, keepdims=True)
    lf_col = _log_sigmoid(jnp.sum(jnp.where(lane == h + heads, gcol, 0.0), axis=1, keepdims=True))
    grow = _soft_cap(gr_ref[...] + br_ref[...])
    sub = lax.broadcasted_iota(jnp.int32, grow.shape, 0)
    li_row = jnp.sum(jnp.where(sub == h, grow, 0.0), axis=0, keepdims=True)
    lf_row = _log_sigmoid(jnp.sum(jnp.where(sub == h + heads, grow, 0.0), axis=0, keepdims=True))

    r_i = lax.broadcasted_iota(jnp.int32, (L, L), 0)
    c_i = lax.broadcasted_iota(jnp.int32, (L, L), 1)
    causal = c_i <= r_i
    tri = jnp.where(causal, 1.0, 0.0).astype(BF16)
    triu = jnp.where(r_i <= c_i, 1.0, 0.0).astype(BF16)
    lf_col_b = jnp.broadcast_to(lf_col, (L, LANES))
    b_col = sum(jnp.dot(tri, t, preferred_element_type=F32) for t in _split3(lf_col_b))[:, :1]
    lf_row_b = jnp.broadcast_to(lf_row, (SUBLANES, L))
    b_row = sum(jnp.dot(t, triu, preferred_element_type=F32) for t in _split3(lf_row_b))[:1, :]
    g_tot = b_col[L - 1:L, :]

    m_prev = m_ref[...]
    d = jnp.where(causal, b_col - b_row + li_row, -jnp.inf)
    m_inter = b_col + m_prev
    m_t = jnp.maximum(m_inter, jnp.max(d, axis=1, keepdims=True))
    w_inter = jnp.exp(m_inter - m_t)

    q = q_ref[...] * (dk ** -0.5)
    k = k_ref[...]
    v = v_ref[...]
    ones = jnp.ones((L, LANES), BF16)
    s = lax.dot_general(q, k, (((1,), (1,)), ((), ())), preferred_element_type=F32)
    a = (jnp.exp(d - m_t) * s).astype(BF16)
    num = (w_inter * jnp.dot(q, c_ref[...].astype(BF16), preferred_element_type=F32)
           + jnp.dot(a, v, preferred_element_type=F32))
    den = (w_inter * jnp.dot(q, n_ref[...].astype(BF16), preferred_element_type=F32)[:, :1]
           + jnp.dot(a, ones, preferred_element_type=F32)[:, :1])
    hh = num / jnp.maximum(jnp.abs(den), jnp.exp(-m_t))

    w_src = g_tot - b_col + li_col
    m_new = jnp.maximum(g_tot + m_prev, jnp.max(w_src, axis=0, keepdims=True))
    decay = jnp.exp(g_tot + m_prev - m_new)
    kw = (k.astype(F32) * jnp.exp(w_src - m_new)).astype(BF16)
    contract_rows = (((0,), (0,)), ((), ()))
    c_ref[...] = decay * c_ref[...] + lax.dot_general(kw, v, contract_rows, preferred_element_type=F32)
    n_ref[...] = decay * n_ref[...] + lax.dot_general(kw, ones, contract_rows, preferred_element_type=F32)
    m_ref[...] = m_new

    hn = hh * lax.rsqrt(jnp.mean(hh * hh, axis=1, keepdims=True) + RMS_EPS)
    o_ref[...] = (jax.nn.sigmoid(op_ref[...].astype(F32)) * (hn * ng_ref[...])).astype(o_ref.dtype)


def _mlstm_recurrence(proj, gates, b_gates, norm_g, batch, seq, heads, dk, dv, chunk):
    assert seq % chunk == 0 and dv % dk == 0
    nc = seq // chunk
    kernel = functools.partial(_mlstm_kernel, heads=heads)
    v0 = 2 * heads * dk // dv
    o0 = v0 + heads
    gates_t = gates.T
    return pl.pallas_call(
        kernel,
        grid=(batch, heads, nc),
        in_specs=[pl.BlockSpec((chunk, dk), lambda b, h, c: (b * nc + c, h)),
                  pl.BlockSpec((chunk, dk), lambda b, h, c: (b * nc + c, heads + h)),
                  pl.BlockSpec((chunk, dv), lambda b, h, c: (b * nc + c, v0 + h)),
                  pl.BlockSpec((chunk, dv), lambda b, h, c: (b * nc + c, o0 + h)),
                  pl.BlockSpec((chunk, 2 * heads), lambda b, h, c: (b * nc + c, 0)),
                  pl.BlockSpec((2 * heads, chunk), lambda b, h, c: (0, b * nc + c)),
                  pl.BlockSpec((1, 2 * heads), lambda b, h, c: (0, 0)),
                  pl.BlockSpec((2 * heads, 1), lambda b, h, c: (0, 0)),
                  pl.BlockSpec((1, dv), lambda b, h, c: (0, h))],
        out_specs=pl.BlockSpec((chunk, dv), lambda b, h, c: (b * nc + c, h)),
        out_shape=jax.ShapeDtypeStruct((batch * seq, heads * dv), BF16),
        scratch_shapes=[pltpu.VMEM((dk, dv), F32), pltpu.VMEM((dk, LANES), F32), pltpu.VMEM((1, 1), F32)],
        compiler_params=_params(("parallel", "parallel", "arbitrary")),
        name="mlstm_recurrence",
    )(proj, proj, proj, proj, gates, gates_t, b_gates.reshape(1, -1), b_gates.reshape(-1, 1),
      norm_g.reshape(1, -1))


def _router_kernel(x_ref, w_ref, b_ref, idx_ref, p_ref, *, n_experts, topk):
    x = x_ref[...]
    w = w_ref[...]
    x_hi = x.astype(BF16)
    x_lo = (x - x_hi.astype(F32)).astype(BF16)
    w_hi = w.astype(BF16)
    w_lo = (w - w_hi.astype(F32)).astype(BF16)
    logits = (jnp.dot(x_hi, w_hi, preferred_element_type=F32) + jnp.dot(x_lo, w_hi, preferred_element_type=F32)
              + jnp.dot(x_hi, w_lo, preferred_element_type=F32)) + b_ref[...]
    lane = lax.broadcasted_iota(jnp.int32, logits.shape, 1)
    lane_f = lane.astype(F32)
    work = jnp.where(lane < n_experts, logits, -jnp.inf)
    idx_out = jnp.zeros(logits.shape, jnp.int32)
    val_out = jnp.full(logits.shape, -jnp.inf, F32)
    for kk in range(topk):
        mx = jnp.max(work, axis=1, keepdims=True)
        first = jnp.min(jnp.where(work == mx, lane_f, float(LANES)), axis=1, keepdims=True)
        first_i = first.astype(jnp.int32)
        idx_out = jnp.where(lane == kk, first_i, idx_out)
        val_out = jnp.where(lane == kk, mx, val_out)
        work = jnp.where(lane == first_i, -jnp.inf, work)
    e = jnp.exp(val_out - jnp.max(val_out, axis=1, keepdims=True))
    idx_ref[...] = idx_out
    p_ref[...] = e / jnp.sum(e, axis=1, keepdims=True)


def _router(x, w, b, n_experts, topk, tm=512):
    M, D = x.shape
    tm = _pick(M, tm)
    w_pad = jnp.zeros((D, LANES), F32).at[:, :n_experts].set(w)
    b_pad = jnp.zeros((1, LANES), F32).at[0, :n_experts].set(b)
    kernel = functools.partial(_router_kernel, n_experts=n_experts, topk=topk)
    out = pl.BlockSpec((tm, LANES), lambda i: (i, 0))
    return pl.pallas_call(
        kernel,
        grid=(M // tm,),
        in_specs=[pl.BlockSpec((tm, D), lambda i: (i, 0)),
                  pl.BlockSpec((D, LANES), lambda i: (0, 0)),
                  pl.BlockSpec((1, LANES), lambda i: (0, 0))],
        out_specs=[out, out],
        out_shape=[jax.ShapeDtypeStruct((M, LANES), jnp.int32), jax.ShapeDtypeStruct((M, LANES), F32)],
        compiler_params=_params(("parallel",)),
        name="moe_router",
    )(x, w_pad, b_pad)


def _row_copy(src_hbm, row, dst_vmem, dst_row, sem):
    return pltpu.make_async_copy(src_hbm.at[pl.ds(row, 1), :], dst_vmem.at[pl.ds(dst_row, 1), :], sem)


def _expert_kernel(te_ref, nt_ref, tok_ref, x_hbm, wgu_ref, bgu_ref, wd_ref, bd_ref, y_ref, xbuf, sem, *, ff):
    t = pl.program_id(0)
    tm = xbuf.shape[0]

    @pl.when(t < nt_ref[0])
    def _():
        def start(r, carry):
            _row_copy(x_hbm, tok_ref[0, 0, r], xbuf, r, sem).start()
            return carry
        lax.fori_loop(0, tm, start, 0)

        def wait(r, carry):
            _row_copy(x_hbm, 0, xbuf, r, sem).wait()
            return carry
        lax.fori_loop(0, tm, wait, 0)

        x = xbuf[...].astype(BF16)
        gu = jnp.dot(x, wgu_ref[0], preferred_element_type=F32) + bgu_ref[0]
        gate = jnp.minimum(gu[:, :ff], SWIGLU_LIMIT)
        up = jnp.clip(gu[:, ff:], -SWIGLU_LIMIT, SWIGLU_LIMIT)
        act = (up + 1.0) * gate * jax.nn.sigmoid(SWIGLU_ALPHA * gate)
        y_ref[...] = jnp.dot(act.astype(BF16), wd_ref[0], preferred_element_type=F32) + bd_ref[0]

    @pl.when(t >= nt_ref[0])
    def _():
        y_ref[...] = jnp.zeros_like(y_ref)


def _expert_ffn(x, tile_expert, n_tiles, tok_rows, w_gate_up, b_gate_up, w_down, b_down, layer, tm):
    N, D = x.shape
    n_layers, E, _, two_ff = w_gate_up.shape
    ff = two_ff // 2
    n_tile_max = tile_expert.shape[0]
    kernel = functools.partial(_expert_kernel, ff=ff)
    grid_spec = pltpu.PrefetchScalarGridSpec(
        num_scalar_prefetch=2,
        grid=(n_tile_max,),
        in_specs=[pl.BlockSpec((1, 1, tm), lambda t, te, nt: (t, 0, 0), memory_space=pltpu.SMEM),
                  pl.BlockSpec(memory_space=pl.ANY),
                  pl.BlockSpec((None, 1, D, two_ff), lambda t, te, nt: (layer, te[t], 0, 0)),
                  pl.BlockSpec((None, 1, 1, two_ff), lambda t, te, nt: (layer, te[t], 0, 0)),
                  pl.BlockSpec((None, 1, ff, D), lambda t, te, nt: (layer, te[t], 0, 0)),
                  pl.BlockSpec((None, 1, 1, D), lambda t, te, nt: (layer, te[t], 0, 0))],
        out_specs=pl.BlockSpec((tm, D), lambda t, te, nt: (t, 0)),
        scratch_shapes=[pltpu.VMEM((tm, D), F32), pltpu.SemaphoreType.DMA(())],
    )
    return pl.pallas_call(
        kernel,
        grid_spec=grid_spec,
        out_shape=jax.ShapeDtypeStruct((n_tile_max * tm, D), F32),
        compiler_params=_params(("arbitrary",)),
        name="moe_expert_ffn",
    )(tile_expert, n_tiles, tok_rows.reshape(n_tile_max, 1, tm), x,
      w_gate_up, b_gate_up.reshape(n_layers, E, 1, two_ff), w_down, b_down.reshape(n_layers, E, 1, D))


def _combine_kernel(pos_ref, p_ref, x_ref, y_hbm, g_ref, b_ref, of_ref, ob_ref, buf, sem, *, topk):
    tc = x_ref.shape[0]

    def start(r, carry):
        for kk in range(topk):
            _row_copy(y_hbm, pos_ref[0, 0, r * topk + kk], buf.at[kk], r, sem).start()
        return carry
    lax.fori_loop(0, tc, start, 0)

    def wait(r, carry):
        for kk in range(topk):
            _row_copy(y_hbm, 0, buf.at[kk], r, sem).wait()
        return carry
    lax.fori_loop(0, tc, wait, 0)

    p = p_ref[...]
    z = DN_ALPHA * x_ref[...]
    for kk in range(topk):
        z = z + p[:, kk:kk + 1] * buf[kk]
    out = _layer_norm_rows(z, g_ref[...], b_ref[...])
    of_ref[...] = out
    ob_ref[...] = out.astype(BF16)


def _combine_res_ln(x, y_rows, pos, top_p, g, b, topk, tc):
    N, D = x.shape
    tc = _pick(N, tc)
    kernel = functools.partial(_combine_kernel, topk=topk)
    row = pl.BlockSpec((tc, D), lambda i: (i, 0))
    vec = pl.BlockSpec((1, D), lambda i: (0, 0))
    return pl.pallas_call(
        kernel,
        grid=(N // tc,),
        in_specs=[pl.BlockSpec((1, 1, tc * topk), lambda i: (i, 0, 0), memory_space=pltpu.SMEM),
                  pl.BlockSpec((tc, LANES), lambda i: (i, 0)),
                  row,
                  pl.BlockSpec(memory_space=pl.ANY),
                  vec, vec],
        out_specs=[row, row],
        out_shape=[jax.ShapeDtypeStruct((N, D), F32), jax.ShapeDtypeStruct((N, D), BF16)],
        scratch_shapes=[pltpu.VMEM((topk, tc, D), F32), pltpu.SemaphoreType.DMA(())],
        compiler_params=_params(("arbitrary",)),
        name="moe_combine_layernorm",
    )(pos.reshape(N // tc, 1, tc * topk), top_p, x, y_rows, g.reshape(1, D), b.reshape(1, D))


def _routing_tables(top_idx, n_experts, tm):
    N, K = top_idx.shape
    n_assign = N * K
    n_tile_max = n_assign // tm + n_experts
    e_flat = top_idx.reshape(-1)
    onehot = (e_flat[:, None] == jnp.arange(n_experts, dtype=jnp.int32)[None, :]).astype(jnp.int32)
    running = jnp.cumsum(onehot, axis=0)
    sizes = running[-1]
    rank = jnp.sum(running * onehot, axis=1) - 1
    padded = (sizes + tm - 1) // tm * tm
    pstart = jnp.cumsum(padded) - padded
    pos = pstart[e_flat] + rank
    n_tiles = (jnp.sum(padded) // tm).astype(jnp.int32).reshape(1)
    tile_start = jnp.arange(n_tile_max, dtype=jnp.int32) * tm
    pend = pstart + padded
    tile_expert = jnp.minimum(jnp.sum((tile_start[:, None] >= pend[None, :]).astype(jnp.int32), axis=1),
                              n_experts - 1).astype(jnp.int32)
    tok_rows = jnp.zeros((n_tile_max * tm,), jnp.int32).at[pos].set(
        jnp.arange(n_assign, dtype=jnp.int32) // K)
    return tile_expert, n_tiles, tok_rows, pos.reshape(N, K).astype(jnp.int32)


def _moe_layer(x_f32, router_w, router_b, w_gate_up, b_gate_up, w_down, b_down, layer, ln_g, ln_b,
               topk=MOE_TOPK, tm=MOE_ROW_TILE, tc=COMBINE_ROW_TILE):
    n_experts = router_w.shape[1]
    top_idx, top_p = _router(x_f32, router_w, router_b, n_experts, topk)
    tile_expert, n_tiles, tok_rows, pos = _routing_tables(top_idx[:, :topk], n_experts, tm)
    y_rows = _expert_ffn(x_f32, tile_expert, n_tiles, tok_rows, w_gate_up, b_gate_up, w_down, b_down, layer, tm)
    return _combine_res_ln(x_f32, y_rows, pos, top_p, ln_g, ln_b, topk, tc)


def _moba_mixer(x_bf16, w_in, w_out, layer, batch, seq, heads=MOBA_HEADS, hd=MOBA_HEAD_DIM, bs=MOBA_BLOCK,
                topk=MOBA_TOPK):
    qkv = _matmul(x_bf16, w_in, layer, BF16)
    o = _moba_attention(qkv, batch, seq, heads, hd, bs, topk)
    return _matmul(o, w_out, layer, F32)


def _mlstm_mixer(x_bf16, w_in, b_gates, norm_g, w_out, layer, batch, seq, heads=MLSTM_HEADS, dk=MLSTM_QK_DIM,
                 dv=MLSTM_V_DIM, chunk=MLSTM_KERNEL_CHUNK):
    n_main = 2 * heads * (dk + dv)
    proj = _matmul(x_bf16, w_in, layer, BF16, n_cols=n_main)
    w_gates = jnp.zeros((1, w_in.shape[1], LANES), F32).at[0, :, :2 * heads].set(w_in[layer, :, n_main:])
    gates = _matmul(x_bf16, w_gates, 0, F32)[:, :2 * heads]
    hg = _mlstm_recurrence(proj, gates, b_gates, norm_g, batch, seq, heads, dk, dv, min(chunk, seq))
    return _matmul(hg, w_out, layer, F32)


def kernel(x, moba_w_in, moba_w_out, mlstm_w_in, mlstm_b_gates, mlstm_norm_g, mlstm_w_out, ln_g, ln_b,
           router_w, router_b, moe_w_gate_up, moe_b_gate_up, moe_w_down, moe_b_down):
    batch, seq, d = x.shape
    x_f32 = x.reshape(batch * seq, d)
    x_bf16 = x_f32.astype(BF16)
    w_gate_up = moe_w_gate_up.astype(BF16)
    w_down = moe_w_down.astype(BF16)
    for i in range(DEPTH):
        slot = i // 2
        if i % 2 == 0:
            y = _moba_mixer(x_bf16, moba_w_in, moba_w_out, slot, batch, seq)
        else:
            y = _mlstm_mixer(x_bf16, mlstm_w_in, mlstm_b_gates[slot], mlstm_norm_g[slot], mlstm_w_out, slot,
                             batch, seq)
        x_f32, x_bf16 = _res_ln(x_f32, y, ln_g[i, 0], ln_b[i, 0])
        x_f32, x_bf16 = _moe_layer(x_f32, router_w[i], router_b[i], w_gate_up, moe_b_gate_up, w_down, moe_b_down, i,
                                   ln_g[i, 1], ln_b[i, 1])
    return x_f32.reshape(batch, seq, d)
```

```python
import functools

import jax
import jax.numpy as jnp
from jax import lax
from jax.experimental import pallas as pl
from jax.experimental.pallas import tpu as pltpu

F32 = jnp.float32
BF16 = jnp.bfloat16

D_MODEL = 4096
DEPTH = 4
MOBA_HEADS = 32
MOBA_HEAD_DIM = D_MODEL // MOBA_HEADS
MOBA_BLOCK = 256
MOBA_TOPK = 3
MLSTM_HEADS = 8
MLSTM_QK_DIM = D_MODEL // 2 // MLSTM_HEADS
MLSTM_V_DIM = D_MODEL // MLSTM_HEADS
MLSTM_GATE_CAP = 15.0
N_EXPERTS = 32
MOE_TOPK = 4
EXPERT_FF = 384
SWIGLU_LIMIT = 7.0
SWIGLU_ALPHA = 1.702
DN_ALPHA = (2 * DEPTH) ** 0.25
LN_EPS = 1e-5
RMS_EPS = 1e-6
LOG2_E = 1.4426950408889634

LANES = 128
SUBLANES = 8
VMEM_LIMIT_BYTES = 56 * 1024 * 1024

MLSTM_KERNEL_CHUNK = 256
MOE_ROW_TILE = 256
COMBINE_ROW_TILE = 128


def _params(semantics):
    return pltpu.CompilerParams(dimension_semantics=semantics, vmem_limit_bytes=VMEM_LIMIT_BYTES)


def _pick(n, target):
    t = min(n, target)
    while n % t:
        t //= 2
    return t


def _matmul_kernel(a_ref, b_ref, o_ref):
    o_ref[...] = jnp.dot(a_ref[...], b_ref[...].astype(BF16), preferred_element_type=F32).astype(o_ref.dtype)


def _matmul(a, b, layer, out_dtype, n_cols=None, tm=1024, tn=512):
    M, K = a.shape
    N = b.shape[2] if n_cols is None else n_cols
    tm, tn = _pick(M, tm), _pick(N, tn)
    return pl.pallas_call(
        _matmul_kernel,
        grid=(M // tm, N // tn),
        in_specs=[pl.BlockSpec((tm, K), lambda i, j: (i, 0)),
                  pl.BlockSpec((None, K, tn), lambda i, j: (layer, 0, j))],
        out_specs=pl.BlockSpec((tm, tn), lambda i, j: (i, j)),
        out_shape=jax.ShapeDtypeStruct((M, N), out_dtype),
        compiler_params=_params(("parallel", "parallel")),
        name="dense_matmul",
    )(a, b)


def _layer_norm_rows(z, g, b):
    mu = jnp.mean(z, axis=-1, keepdims=True)
    zc = z - mu
    var = jnp.mean(zc * zc, axis=-1, keepdims=True)
    return zc * lax.rsqrt(var + LN_EPS) * g + b


def _res_ln_kernel(x_ref, y_ref, g_ref, b_ref, of_ref, ob_ref):
    z = DN_ALPHA * x_ref[...] + y_ref[...].astype(F32)
    out = _layer_norm_rows(z, g_ref[...], b_ref[...])
    of_ref[...] = out
    ob_ref[...] = out.astype(BF16)


def _res_ln(x, y, g, b, tm=256):
    M, D = x.shape
    tm = _pick(M, tm)
    row = pl.BlockSpec((tm, D), lambda i: (i, 0))
    vec = pl.BlockSpec((1, D), lambda i: (0, 0))
    return pl.pallas_call(
        _res_ln_kernel,
        grid=(M // tm,),
        in_specs=[row, row, vec, vec],
        out_specs=[row, row],
        out_shape=[jax.ShapeDtypeStruct((M, D), F32), jax.ShapeDtypeStruct((M, D), BF16)],
        compiler_params=_params(("parallel",)),
        name="residual_layernorm",
    )(x, y, g.reshape(1, D), b.reshape(1, D))


def _moba_kernel(q_ref, k_ref, v_ref, o_ref, kmean_ref, vt_ref, s_ref, p_ref, *, nb, bs, topk, scale):
    c = scale * LOG2_E
    nt = (((1,), (1,)), ((), ()))
    neg_inf = -jnp.inf

    kmean_ref[...] = jnp.zeros_like(kmean_ref)
    for n in range(nb):
        blk = slice(n * bs, (n + 1) * bs)
        kmean_ref[n:n + 1, :] = jnp.mean(k_ref[blk, :].astype(F32), axis=0, keepdims=True)
        vt_ref[:, blk] = v_ref[blk, :].T

    kmean = kmean_ref[...]
    km_hi = kmean.astype(BF16)
    km_lo = (kmean - km_hi.astype(F32)).astype(BF16)
    q_all = q_ref[...]
    gate = (lax.dot_general(km_hi, q_all, nt, preferred_element_type=F32)
            + lax.dot_general(km_lo, q_all, nt, preferred_element_type=F32))

    def scores(j):
        s_ref[j % 2, 0:(j + 1) * bs, :] = lax.dot_general(
            k_ref[0:(j + 1) * bs, :], q_ref[j * bs:(j + 1) * bs, :], nt, preferred_element_type=F32)

    def softmax(j):
        slot = j % 2
        own = slice(j * bs, (j + 1) * bs)
        key_i = lax.broadcasted_iota(jnp.int32, (bs, bs), 0)
        qry_i = lax.broadcasted_iota(jnp.int32, (bs, bs), 1)
        causal = key_i <= qry_i
        m = jnp.max(jnp.where(causal, s_ref[slot, own, :], neg_inf), axis=0, keepdims=True)
        sel = [None] * j
        if j > topk:
            g = gate[:, own]
            blk_i = lax.broadcasted_iota(jnp.int32, g.shape, 0)
            g = jnp.where(blk_i < j, g, neg_inf)
        for n in range(j):
            bm = jnp.max(s_ref[slot, n * bs:(n + 1) * bs, :], axis=0, keepdims=True)
            if j > topk:
                gn = g[n:n + 1, :]
                beats = (g > gn) | ((g == gn) & (blk_i < n))
                sel[n] = jnp.sum(beats.astype(F32), axis=0, keepdims=True) < float(topk)
                bm = jnp.where(sel[n], bm, neg_inf)
            m = jnp.maximum(m, bm)
        p = jnp.exp2((jnp.where(causal, s_ref[slot, own, :], neg_inf) - m) * c)
        l = jnp.sum(p, axis=0, keepdims=True)
        p_ref[slot, own, :] = p.astype(BF16)
        for n in range(j):
            blk = slice(n * bs, (n + 1) * bs)
            m_use = m if sel[n] is None else jnp.where(sel[n], m, jnp.inf)
            p = jnp.exp2((s_ref[slot, blk, :] - m_use) * c)
            l = l + jnp.sum(p, axis=0, keepdims=True)
            p_ref[slot, blk, :] = p.astype(BF16)
        return l

    scores(0)
    for j in range(nb):
        if j + 1 < nb:
            scores(j + 1)
        l = softmax(j)
        nk = (j + 1) * bs
        acc = jnp.dot(vt_ref[:, 0:nk], p_ref[j % 2, 0:nk, :], preferred_element_type=F32)
        o_ref[j * bs:(j + 1) * bs, :] = (acc * (1.0 / l)).T.astype(o_ref.dtype)


def _moba_attention(qkv, batch, seq, heads, hd, bs, topk):
    assert seq % bs == 0 and hd % LANES == 0 and bs % LANES == 0
    nb = seq // bs
    nb_pad = -(-nb // SUBLANES) * SUBLANES
    kernel = functools.partial(_moba_kernel, nb=nb, bs=bs, topk=min(topk, nb), scale=hd ** -0.5)
    return pl.pallas_call(
        kernel,
        grid=(batch, heads),
        in_specs=[pl.BlockSpec((seq, hd), lambda b, h: (b, h)),
                  pl.BlockSpec((seq, hd), lambda b, h: (b, heads + h)),
                  pl.BlockSpec((seq, hd), lambda b, h: (b, 2 * heads + h))],
        out_specs=pl.BlockSpec((seq, hd), lambda b, h: (b, h)),
        out_shape=jax.ShapeDtypeStruct((batch * seq, heads * hd), BF16),
        scratch_shapes=[pltpu.VMEM((nb_pad, hd), F32), pltpu.VMEM((hd, seq), BF16),
                        pltpu.VMEM((2, seq, bs), F32), pltpu.VMEM((2, seq, bs), BF16)],
        compiler_params=_params(("parallel", "parallel")),
        name="moba_attention",
    )(qkv, qkv, qkv)


def _split3(x):
    h1 = x.astype(BF16)
    r1 = x - h1.astype(F32)
    h2 = r1.astype(BF16)
    r2 = r1 - h2.astype(F32)
    return h1, h2, r2.astype(BF16)


def _soft_cap(z):
    return MLSTM_GATE_CAP * jnp.tanh(z / MLSTM_GATE_CAP)


def _log_sigmoid(z):
    return jnp.minimum(z, 0.0) - jnp.log1p(jnp.exp(-jnp.abs(z)))


def _mlstm_kernel(q_ref, k_ref, v_ref, op_ref, gc_ref, gr_ref, bc_ref, br_ref, ng_ref, o_ref,
                  c_ref, n_ref, m_ref, *, heads):
    h = pl.program_id(1)
    c = pl.program_id(2)
    L, dk = q_ref.shape

    @pl.when(c == 0)
    def _():
        c_ref[...] = jnp.zeros_like(c_ref)
        n_ref[...] = jnp.zeros_like(n_ref)
        m_ref[...] = jnp.zeros_like(m_ref)

    gcol = _soft_cap(gc_ref[...] + bc_ref[...])
    lane = lax.broadcasted_iota(jnp.int32, gcol.shape, 1)
    li_col = jnp.sum(jnp.where(lane == h, gcol, 0.0), axis=1, keepdims=True)
    lf_col = _log_sigmoid(jnp.sum(jnp.where(lane == h + heads, gcol, 0.0), axis=1, keepdims=True))
    grow = _soft_cap(gr_ref[...] + br_ref[...])
    sub = lax.broadcasted_iota(jnp.int32, grow.shape, 0)
    li_row = jnp.sum(jnp.where(sub == h, grow, 0.0), axis=0, keepdims=True)
    lf_row = _log_sigmoid(jnp.sum(jnp.where(sub == h + heads, grow, 0.0), axis=0, keepdims=True))

    r_i = lax.broadcasted_iota(jnp.int32, (L, L), 0)
    c_i = lax.broadcasted_iota(jnp.int32, (L, L), 1)
    causal = c_i <= r_i
    tri = jnp.where(causal, 1.0, 0.0).astype(BF16)
    triu = jnp.where(r_i <= c_i, 1.0, 0.0).astype(BF16)
    lf_col_b = jnp.broadcast_to(lf_col, (L, LANES))
    b_col = sum(jnp.dot(tri, t, preferred_element_type=F32) for t in _split3(lf_col_b))[:, :1]
    lf_row_b = jnp.broadcast_to(lf_row, (SUBLANES, L))
    b_row = sum(jnp.dot(t, triu, preferred_element_type=F32) for t in _split3(lf_row_b))[:1, :]
    g_tot = b_col[L - 1:L, :]

    m_prev = m_ref[...]
    d = jnp.where(causal, b_col - b_row + li_row, -jnp.inf)
    m_inter = b_col + m_prev
    m_t = jnp.maximum(m_inter, jnp.max(d, axis=1, keepdims=True))
    w_inter = jnp.exp(m_inter - m_t)

    q = q_ref[...] * (dk ** -0.5)
    k = k_ref[...]
    v = v_ref[...]
    ones = jnp.ones((L, LANES), BF16)
    s = lax.dot_general(q, k, (((1,), (1,)), ((), ())), preferred_element_type=F32)
    a = (jnp.exp(d - m_t) * s).astype(BF16)
    num = (w_inter * jnp.dot(q, c_ref[...].astype(BF16), preferred_element_type=F32)
           + jnp.dot(a, v, preferred_element_type=F32))
    den = (w_inter * jnp.dot(q, n_ref[...].astype(BF16), preferred_element_type=F32)[:, :1]
           + jnp.dot(a, ones, preferred_element_type=F32)[:, :1])
    hh = num / jnp.maximum(jnp.abs(den), jnp.exp(-m_t))

    w_src = g_tot - b_col + li_col
    m_new = jnp.maximum(g_tot + m_prev, jnp.max(w_src, axis=0, keepdims=True))
    decay = jnp.exp(g_tot + m_prev - m_new)
    kw = (k.astype(F32) * jnp.exp(w_src - m_new)).astype(BF16)
    contract_rows = (((0,), (0,)), ((), ()))
    c_ref[...] = decay * c_ref[...] + lax.dot_general(kw, v, contract_rows, preferred_element_type=F32)
    n_ref[...] = decay * n_ref[...] + lax.dot_general(kw, ones, contract_rows, preferred_element_type=F32)
    m_ref[...] = m_new

    hn = hh * lax.rsqrt(jnp.mean(hh * hh, axis=1, keepdims=True) + RMS_EPS)
    o_ref[...] = (jax.nn.sigmoid(op_ref[...].astype(F32)) * (hn * ng_ref[...])).astype(o_ref.dtype)


def _mlstm_recurrence(proj, gates, b_gates, norm_g, batch, seq, heads, dk, dv, chunk):
    assert seq % chunk == 0 and dv % dk == 0
    nc = seq // chunk
    kernel = functools.partial(_mlstm_kernel, heads=heads)
    v0 = 2 * heads * dk // dv
    o0 = v0 + heads
    gates_t = gates.T
    return pl.pallas_call(
        kernel,
        grid=(batch, heads, nc),
        in_specs=[pl.BlockSpec((chunk, dk), lambda b, h, c: (b * nc + c, h)),
                  pl.BlockSpec((chunk, dk), lambda b, h, c: (b * nc + c, heads + h)),
                  pl.BlockSpec((chunk, dv), lambda b, h, c: (b * nc + c, v0 + h)),
                  pl.BlockSpec((chunk, dv), lambda b, h, c: (b * nc + c, o0 + h)),
                  pl.BlockSpec((chunk, 2 * heads), lambda b, h, c: (b * nc + c, 0)),
                  pl.BlockSpec((2 * heads, chunk), lambda b, h, c: (0, b * nc + c)),
                  pl.BlockSpec((1, 2 * heads), lambda b, h, c: (0, 0)),
                  pl.BlockSpec((2 * heads, 1), lambda b, h, c: (0, 0)),
                  pl.BlockSpec((1, dv), lambda b, h, c: (0, h))],
        out_specs=pl.BlockSpec((chunk, dv), lambda b, h, c: (b * nc + c, h)),
        out_shape=jax.ShapeDtypeStruct((batch * seq, heads * dv), BF16),
        scratch_shapes=[pltpu.VMEM((dk, dv), F32), pltpu.VMEM((dk, LANES), F32), pltpu.VMEM((1, 1), F32)],
        compiler_params=_params(("parallel", "parallel", "arbitrary")),
        name="mlstm_recurrence",
    )(proj, proj, proj, proj, gates, gates_t, b_gates.reshape(1, -1), b_gates.reshape(-1, 1),
      norm_g.reshape(1, -1))


def _router_kernel(x_ref, w_ref, b_ref, idx_ref, p_ref, *, n_experts, topk):
    x = x_ref[...]
    w = w_ref[...]
    x_hi = x.astype(BF16)
    x_lo = (x - x_hi.astype(F32)).astype(BF16)
    w_hi = w.astype(BF16)
    w_lo = (w - w_hi.astype(F32)).astype(BF16)
    logits = (jnp.dot(x_hi, w_hi, preferred_element_type=F32) + jnp.dot(x_lo, w_hi, preferred_element_type=F32)
              + jnp.dot(x_hi, w_lo, preferred_element_type=F32)) + b_ref[...]
    lane = lax.broadcasted_iota(jnp.int32, logits.shape, 1)
    lane_f = lane.astype(F32)
    work = jnp.where(lane < n_experts, logits, -jnp.inf)
    idx_out = jnp.zeros(logits.shape, jnp.int32)
    val_out = jnp.full(logits.shape, -jnp.inf, F32)
    for kk in range(topk):
        mx = jnp.max(work, axis=1, keepdims=True)
        first = jnp.min(jnp.where(work == mx, lane_f, float(LANES)), axis=1, keepdims=True)
        first_i = first.astype(jnp.int32)
        idx_out = jnp.where(lane == kk, first_i, idx_out)
        val_out = jnp.where(lane == kk, mx, val_out)
        work = jnp.where(lane == first_i, -jnp.inf, work)
    e = jnp.exp(val_out - jnp.max(val_out, axis=1, keepdims=True))
    idx_ref[...] = idx_out
    p_ref[...] = e / jnp.sum(e, axis=1, keepdims=True)


def _router(x, w, b, n_experts, topk, tm=512):
    M, D = x.shape
    tm = _pick(M, tm)
    w_pad = jnp.zeros((D, LANES), F32).at[:, :n_experts].set(w)
    b_pad = jnp.zeros((1, LANES), F32).at[0, :n_experts].set(b)
    kernel = functools.partial(_router_kernel, n_experts=n_experts, topk=topk)
    out = pl.BlockSpec((tm, LANES), lambda i: (i, 0))
    return pl.pallas_call(
        kernel,
        grid=(M // tm,),
        in_specs=[pl.BlockSpec((tm, D), lambda i: (i, 0)),
                  pl.BlockSpec((D, LANES), lambda i: (0, 0)),
                  pl.BlockSpec((1, LANES), lambda i: (0, 0))],
        out_specs=[out, out],
        out_shape=[jax.ShapeDtypeStruct((M, LANES), jnp.int32), jax.ShapeDtypeStruct((M, LANES), F32)],
        compiler_params=_params(("parallel",)),
        name="moe_router",
    )(x, w_pad, b_pad)


def _start_row_gather(src_hbm, idx_ref, n_rows, dst, sem, unroll=8):
    def body(r, carry):
        pltpu.make_async_copy(src_hbm.at[pl.ds(idx_ref[0, 0, r], 1), :], dst.at[pl.ds(r, 1), :], sem).start()
        return carry
    lax.fori_loop(0, n_rows, body, 0, unroll=unroll)


def _wait_row_gather(src_hbm, dst, sem):
    pltpu.make_async_copy(src_hbm.at[pl.ds(0, dst.shape[0]), :], dst, sem).wait()


def _expert_kernel(te_ref, nt_ref, tok_ref, tok_next_ref, x_hbm, wgu_ref, bgu_ref, wd_ref, bd_ref, y_ref,
                   xbuf, sem, *, ff):
    t = pl.program_id(0)
    n_tiles = nt_ref[0]
    tm = xbuf.shape[1]
    slot = lax.rem(t, 2)

    @pl.when(t == 0)
    def _():
        _start_row_gather(x_hbm, tok_ref, tm, xbuf.at[0], sem.at[0])

    @pl.when(t + 1 < n_tiles)
    def _():
        _start_row_gather(x_hbm, tok_next_ref, tm, xbuf.at[1 - slot], sem.at[1 - slot])

    @pl.when(t < n_tiles)
    def _():
        _wait_row_gather(x_hbm, xbuf.at[slot], sem.at[slot])
        x = xbuf[slot].astype(BF16)
        gu = jnp.dot(x, wgu_ref[0], preferred_element_type=F32) + bgu_ref[0]
        gate = jnp.minimum(gu[:, :ff], SWIGLU_LIMIT)
        up = jnp.clip(gu[:, ff:], -SWIGLU_LIMIT, SWIGLU_LIMIT)
        act = (up + 1.0) * gate * jax.nn.sigmoid(SWIGLU_ALPHA * gate)
        y_ref[...] = jnp.dot(act.astype(BF16), wd_ref[0], preferred_element_type=F32) + bd_ref[0]

    @pl.when(t >= n_tiles)
    def _():
        y_ref[...] = jnp.zeros_like(y_ref)


def _expert_ffn(x, tile_expert, n_tiles, tok_rows, w_gate_up, b_gate_up, w_down, b_down, layer, tm):
    N, D = x.shape
    n_layers, E, _, two_ff = w_gate_up.shape
    ff = two_ff // 2
    n_tile_max = tile_expert.shape[0]
    assert N >= tm
    kernel = functools.partial(_expert_kernel, ff=ff)
    tok_tiles = tok_rows.reshape(n_tile_max, 1, tm)
    grid_spec = pltpu.PrefetchScalarGridSpec(
        num_scalar_prefetch=2,
        grid=(n_tile_max,),
        in_specs=[pl.BlockSpec((1, 1, tm), lambda t, te, nt: (t, 0, 0), memory_space=pltpu.SMEM),
                  pl.BlockSpec((1, 1, tm), lambda t, te, nt: (jnp.minimum(t + 1, n_tile_max - 1), 0, 0),
                               memory_space=pltpu.SMEM),
                  pl.BlockSpec(memory_space=pl.ANY),
                  pl.BlockSpec((None, 1, D, two_ff), lambda t, te, nt: (layer, te[t], 0, 0)),
                  pl.BlockSpec((None, 1, 1, two_ff), lambda t, te, nt: (layer, te[t], 0, 0)),
                  pl.BlockSpec((None, 1, ff, D), lambda t, te, nt: (layer, te[t], 0, 0)),
                  pl.BlockSpec((None, 1, 1, D), lambda t, te, nt: (layer, te[t], 0, 0))],
        out_specs=pl.BlockSpec((tm, D), lambda t, te, nt: (t, 0)),
        scratch_shapes=[pltpu.VMEM((2, tm, D), F32), pltpu.SemaphoreType.DMA((2,))],
    )
    return pl.pallas_call(
        kernel,
        grid_spec=grid_spec,
        out_shape=jax.ShapeDtypeStruct((n_tile_max * tm, D), F32),
        compiler_params=_params(("arbitrary",)),
        name="moe_expert_ffn",
    )(tile_expert, n_tiles, tok_tiles, tok_tiles, x,
      w_gate_up, b_gate_up.reshape(n_layers, E, 1, two_ff), w_down, b_down.reshape(n_layers, E, 1, D))


def _combine_kernel(pos_ref, pos_next_ref, p_ref, x_ref, y_hbm, g_ref, b_ref, of_ref, ob_ref, buf, sem, *, topk):
    i = pl.program_id(0)
    tc = x_ref.shape[0]
    slot = lax.rem(i, 2)

    @pl.when(i == 0)
    def _():
        _start_row_gather(y_hbm, pos_ref, topk * tc, buf.at[0], sem.at[0])

    @pl.when(i + 1 < pl.num_programs(0))
    def _():
        _start_row_gather(y_hbm, pos_next_ref, topk * tc, buf.at[1 - slot], sem.at[1 - slot])

    _wait_row_gather(y_hbm, buf.at[slot], sem.at[slot])
    p = p_ref[...]
    z = DN_ALPHA * x_ref[...]
    for kk in range(topk):
        z = z + p[:, kk:kk + 1] * buf[slot, kk * tc:(kk + 1) * tc, :]
    out = _layer_norm_rows(z, g_ref[...], b_ref[...])
    of_ref[...] = out
    ob_ref[...] = out.astype(BF16)


def _combine_res_ln(x, y_rows, pos, top_p, g, b, topk, tc):
    N, D = x.shape
    tc = _pick(N, tc)
    n_tiles = N // tc
    assert y_rows.shape[0] >= topk * tc
    kernel = functools.partial(_combine_kernel, topk=topk)
    row = pl.BlockSpec((tc, D), lambda i: (i, 0))
    vec = pl.BlockSpec((1, D), lambda i: (0, 0))
    pos_tiles = pos.reshape(n_tiles, tc, topk).transpose(0, 2, 1).reshape(n_tiles, 1, topk * tc)
    return pl.pallas_call(
        kernel,
        grid=(n_tiles,),
        in_specs=[pl.BlockSpec((1, 1, tc * topk), lambda i: (i, 0, 0), memory_space=pltpu.SMEM),
                  pl.BlockSpec((1, 1, tc * topk), lambda i: (jnp.minimum(i + 1, n_tiles - 1), 0, 0),
                               memory_space=pltpu.SMEM),
                  pl.BlockSpec((tc, LANES), lambda i: (i, 0)),
                  row,
                  pl.BlockSpec(memory_space=pl.ANY),
                  vec, vec],
        out_specs=[row, row],
        out_shape=[jax.ShapeDtypeStruct((N, D), F32), jax.ShapeDtypeStruct((N, D), BF16)],
        scratch_shapes=[pltpu.VMEM((2, topk * tc, D), F32), pltpu.SemaphoreType.DMA((2,))],
        compiler_params=_params(("arbitrary",)),
        name="moe_combine_layernorm",
    )(pos_tiles, pos_tiles, top_p, x, y_rows, g.reshape(1, D), b.reshape(1, D))


def _routing_tables(top_idx, n_experts, tm):
    N, K = top_idx.shape
    n_assign = N * K
    n_tile_max = n_assign // tm + n_experts
    e_flat = top_idx.reshape(-1)
    onehot = (e_flat[:, None] == jnp.arange(n_experts, dtype=jnp.int32)[None, :]).astype(jnp.int32)
    running = jnp.cumsum(onehot, axis=0)
    sizes = running[-1]
    rank = jnp.sum(running * onehot, axis=1) - 1
    padded = (sizes + tm - 1) // tm * tm
    pstart = jnp.cumsum(padded) - padded
    pos = pstart[e_flat] + rank
    n_tiles = (jnp.sum(padded) // tm).astype(jnp.int32).reshape(1)
    tile_start = jnp.arange(n_tile_max, dtype=jnp.int32) * tm
    pend = pstart + padded
    tile_expert = jnp.minimum(jnp.sum((tile_start[:, None] >= pend[None, :]).astype(jnp.int32), axis=1),
                              n_experts - 1).astype(jnp.int32)
    tok_rows = jnp.zeros((n_tile_max * tm,), jnp.int32).at[pos].set(
        jnp.arange(n_assign, dtype=jnp.int32) // K)
    return tile_expert, n_tiles, tok_rows, pos.reshape(N, K).astype(jnp.int32)


def _moe_layer(x_f32, router_w, router_b, w_gate_up, b_gate_up, w_down, b_down, layer, ln_g, ln_b,
               topk=MOE_TOPK, tm=MOE_ROW_TILE, tc=COMBINE_ROW_TILE):
    n_experts = router_w.shape[1]
    top_idx, top_p = _router(x_f32, router_w, router_b, n_experts, topk)
    tile_expert, n_tiles, tok_rows, pos = _routing_tables(top_idx[:, :topk], n_experts, tm)
    y_rows = _expert_ffn(x_f32, tile_expert, n_tiles, tok_rows, w_gate_up, b_gate_up, w_down, b_down, layer, tm)
    return _combine_res_ln(x_f32, y_rows, pos, top_p, ln_g, ln_b, topk, tc)


def _moba_mixer(x_bf16, w_in, w_out, layer, batch, seq, heads=MOBA_HEADS, hd=MOBA_HEAD_DIM, bs=MOBA_BLOCK,
                topk=MOBA_TOPK):
    qkv = _matmul(x_bf16, w_in, layer, BF16)
    o = _moba_attention(qkv, batch, seq, heads, hd, bs, topk)
    return _matmul(o, w_out, layer, F32)


def _mlstm_mixer(x_bf16, w_in, b_gates, norm_g, w_out, layer, batch, seq, heads=MLSTM_HEADS, dk=MLSTM_QK_DIM,
                 dv=MLSTM_V_DIM, chunk=MLSTM_KERNEL_CHUNK):
    n_main = 2 * heads * (dk + dv)
    proj = _matmul(x_bf16, w_in, layer, BF16, n_cols=n_main)
    w_gates = jnp.zeros((1, w_in.shape[1], LANES), F32).at[0, :, :2 * heads].set(w_in[layer, :, n_main:])
    gates = _matmul(x_bf16, w_gates, 0, F32)[:, :2 * heads]
    hg = _mlstm_recurrence(proj, gates, b_gates, norm_g, batch, seq, heads, dk, dv, min(chunk, seq))
    return _matmul(hg, w_out, layer, F32)


def kernel(x, moba_w_in, moba_w_out, mlstm_w_in, mlstm_b_gates, mlstm_norm_g, mlstm_w_out, ln_g, ln_b,
           router_w, router_b, moe_w_gate_up, moe_b_gate_up, moe_w_down, moe_b_down):
    batch, seq, d = x.shape
    x_f32 = x.reshape(batch * seq, d)
    x_bf16 = x_f32.astype(BF16)
    w_gate_up = moe_w_gate_up.astype(BF16)
    w_down = moe_w_down.astype(BF16)
    for i in range(DEPTH):
        slot = i // 2
        if i % 2 == 0:
            y = _moba_mixer(x_bf16, moba_w_in, moba_w_out, slot, batch, seq)
        else:
            y = _mlstm_mixer(x_bf16, mlstm_w_in, mlstm_b_gates[slot], mlstm_norm_g[slot], mlstm_w_out, slot,
                             batch, seq)
        x_f32, x_bf16 = _res_ln(x_f32, y, ln_g[i, 0], ln_b[i, 0])
        x_f32, x_bf16 = _moe_layer(x_f32, router_w[i], router_b[i], w_gate_up, moe_b_gate_up, w_down, moe_b_down, i,
                                   ln_g[i, 1], ln_b[i, 1])
    return x_f32.reshape(batch, seq, d)
```

```python
import functools

import jax
import jax.numpy as jnp
from jax import lax
from jax.experimental import pallas as pl
from jax.experimental.pallas import tpu as pltpu

F32 = jnp.float32
BF16 = jnp.bfloat16

D_MODEL = 4096
DEPTH = 4
MOBA_HEADS = 32
MOBA_HEAD_DIM = D_MODEL // MOBA_HEADS
MOBA_BLOCK = 256
MOBA_TOPK = 3
MLSTM_HEADS = 8
MLSTM_QK_DIM = D_MODEL // 2 // MLSTM_HEADS
MLSTM_V_DIM = D_MODEL // MLSTM_HEADS
MLSTM_GATE_CAP = 15.0
N_EXPERTS = 32
MOE_TOPK = 4
EXPERT_FF = 384
SWIGLU_LIMIT = 7.0
SWIGLU_ALPHA = 1.702
DN_ALPHA = (2 * DEPTH) ** 0.25
LN_EPS = 1e-5
RMS_EPS = 1e-6
LOG2_E = 1.4426950408889634

LANES = 128
SUBLANES = 8
VMEM_LIMIT_BYTES = 56 * 1024 * 1024

MLSTM_KERNEL_CHUNK = 256
MOE_ROW_TILE = 256
COMBINE_ROW_TILE = 128


def _params(semantics):
    return pltpu.CompilerParams(dimension_semantics=semantics, vmem_limit_bytes=VMEM_LIMIT_BYTES)


def _pick(n, target):
    t = min(n, target)
    while n % t:
        t //= 2
    return t


def _matmul_kernel(a_ref, b_ref, o_ref):
    o_ref[...] = jnp.dot(a_ref[...], b_ref[...].astype(BF16), preferred_element_type=F32).astype(o_ref.dtype)


def _matmul(a, b, layer, out_dtype, n_cols=None, tm=2048, tn=512):
    M, K = a.shape
    N = b.shape[2] if n_cols is None else n_cols
    tm, tn = _pick(M, tm), _pick(N, tn)
    return pl.pallas_call(
        _matmul_kernel,
        grid=(M // tm, N // tn),
        in_specs=[pl.BlockSpec((tm, K), lambda i, j: (i, 0), pipeline_mode=pl.Buffered(1)),
                  pl.BlockSpec((None, K, tn), lambda i, j: (layer, 0, j))],
        out_specs=pl.BlockSpec((tm, tn), lambda i, j: (i, j)),
        out_shape=jax.ShapeDtypeStruct((M, N), out_dtype),
        compiler_params=_params(("parallel", "parallel")),
        name="dense_matmul",
    )(a, b)


def _layer_norm_rows(z, g, b):
    mu = jnp.mean(z, axis=-1, keepdims=True)
    zc = z - mu
    var = jnp.mean(zc * zc, axis=-1, keepdims=True)
    return zc * lax.rsqrt(var + LN_EPS) * g + b


HIGH_HALF = 0xFFFF0000


def _pack_bf16_pair(lo, hi):
    lo_bits = lax.bitcast_convert_type(lo.astype(BF16).astype(F32), jnp.uint32)
    hi_bits = lax.bitcast_convert_type(hi.astype(BF16).astype(F32), jnp.uint32)
    return (hi_bits & jnp.uint32(HIGH_HALF)) | (lo_bits >> 16)


def _unpack_bf16_pair(w):
    lo = lax.bitcast_convert_type(w << 16, F32)
    hi = lax.bitcast_convert_type(w & jnp.uint32(HIGH_HALF), F32)
    return lo, hi


def _res_ln_kernel(x_ref, y_ref, g_ref, b_ref, of_ref, op_ref):
    half = op_ref.shape[1]
    z = DN_ALPHA * x_ref[...] + y_ref[...].astype(F32)
    out = _layer_norm_rows(z, g_ref[...], b_ref[...])
    of_ref[...] = out
    op_ref[...] = _pack_bf16_pair(out[:, :half], out[:, half:])


def _res_ln(x, y, g, b, tm=256):
    M, D = x.shape
    tm = _pick(M, tm)
    row = pl.BlockSpec((tm, D), lambda i: (i, 0))
    vec = pl.BlockSpec((1, D), lambda i: (0, 0))
    return pl.pallas_call(
        _res_ln_kernel,
        grid=(M // tm,),
        in_specs=[row, row, vec, vec],
        out_specs=[row, pl.BlockSpec((tm, D // 2), lambda i: (i, 0))],
        out_shape=[jax.ShapeDtypeStruct((M, D), F32), jax.ShapeDtypeStruct((M, D // 2), jnp.uint32)],
        compiler_params=_params(("parallel",)),
        name="residual_layernorm",
    )(x, y, g.reshape(1, D), b.reshape(1, D))


def _moba_kernel(q_ref, k_ref, v_ref, o_ref, kmean_ref, vt_ref, s_ref, p_ref, *, nb, bs, topk, scale):
    c = scale * LOG2_E
    nt = (((1,), (1,)), ((), ()))
    neg_inf = -jnp.inf

    kmean_ref[...] = jnp.zeros_like(kmean_ref)
    for n in range(nb):
        blk = slice(n * bs, (n + 1) * bs)
        kmean_ref[n:n + 1, :] = jnp.mean(k_ref[blk, :].astype(F32), axis=0, keepdims=True)
        vt_ref[:, blk] = v_ref[blk, :].T

    kmean = kmean_ref[...]
    km_hi = kmean.astype(BF16)
    km_lo = (kmean - km_hi.astype(F32)).astype(BF16)
    q_all = q_ref[...]
    gate = (lax.dot_general(km_hi, q_all, nt, preferred_element_type=F32)
            + lax.dot_general(km_lo, q_all, nt, preferred_element_type=F32))

    def scores(j):
        s_ref[j % 2, 0:(j + 1) * bs, :] = lax.dot_general(
            k_ref[0:(j + 1) * bs, :], q_ref[j * bs:(j + 1) * bs, :], nt, preferred_element_type=F32)

    def softmax(j):
        slot = j % 2
        own = slice(j * bs, (j + 1) * bs)
        key_i = lax.broadcasted_iota(jnp.int32, (bs, bs), 0)
        qry_i = lax.broadcasted_iota(jnp.int32, (bs, bs), 1)
        causal = key_i <= qry_i
        m = jnp.max(jnp.where(causal, s_ref[slot, own, :], neg_inf), axis=0, keepdims=True)
        sel = [None] * j
        if j > topk:
            g = gate[:, own]
            blk_i = lax.broadcasted_iota(jnp.int32, g.shape, 0)
            g = jnp.where(blk_i < j, g, neg_inf)
        for n in range(j):
            bm = jnp.max(s_ref[slot, n * bs:(n + 1) * bs, :], axis=0, keepdims=True)
            if j > topk:
                gn = g[n:n + 1, :]
                beats = (g > gn) | ((g == gn) & (blk_i < n))
                sel[n] = jnp.sum(beats.astype(F32), axis=0, keepdims=True) < float(topk)
                bm = jnp.where(sel[n], bm, neg_inf)
            m = jnp.maximum(m, bm)
        p = jnp.exp2((jnp.where(causal, s_ref[slot, own, :], neg_inf) - m) * c)
        l = jnp.sum(p, axis=0, keepdims=True)
        p_ref[slot, own, :] = p.astype(BF16)
        for n in range(j):
            blk = slice(n * bs, (n + 1) * bs)
            m_use = m if sel[n] is None else jnp.where(sel[n], m, jnp.inf)
            p = jnp.exp2((s_ref[slot, blk, :] - m_use) * c)
            l = l + jnp.sum(p, axis=0, keepdims=True)
            p_ref[slot, blk, :] = p.astype(BF16)
        return l

    scores(0)
    for j in range(nb):
        if j + 1 < nb:
            scores(j + 1)
        l = softmax(j)
        nk = (j + 1) * bs
        acc = jnp.dot(vt_ref[:, 0:nk], p_ref[j % 2, 0:nk, :], preferred_element_type=F32)
        o_ref[j * bs:(j + 1) * bs, :] = (acc * (1.0 / l)).T.astype(o_ref.dtype)


def _moba_attention(qkv, batch, seq, heads, hd, bs, topk):
    assert seq % bs == 0 and hd % LANES == 0 and bs % LANES == 0
    nb = seq // bs
    nb_pad = -(-nb // SUBLANES) * SUBLANES
    kernel = functools.partial(_moba_kernel, nb=nb, bs=bs, topk=min(topk, nb), scale=hd ** -0.5)
    return pl.pallas_call(
        kernel,
        grid=(batch, heads),
        in_specs=[pl.BlockSpec((seq, hd), lambda b, h: (b, h)),
                  pl.BlockSpec((seq, hd), lambda b, h: (b, heads + h)),
                  pl.BlockSpec((seq, hd), lambda b, h: (b, 2 * heads + h))],
        out_specs=pl.BlockSpec((seq, hd), lambda b, h: (b, h)),
        out_shape=jax.ShapeDtypeStruct((batch * seq, heads * hd), BF16),
        scratch_shapes=[pltpu.VMEM((nb_pad, hd), F32), pltpu.VMEM((hd, seq), BF16),
                        pltpu.VMEM((2, seq, bs), F32), pltpu.VMEM((2, seq, bs), BF16)],
        compiler_params=_params(("parallel", "parallel")),
        name="moba_attention",
    )(qkv, qkv, qkv)


def _split3(x):
    h1 = x.astype(BF16)
    r1 = x - h1.astype(F32)
    h2 = r1.astype(BF16)
    r2 = r1 - h2.astype(F32)
    return h1, h2, r2.astype(BF16)


def _soft_cap(z):
    return MLSTM_GATE_CAP * jnp.tanh(z / MLSTM_GATE_CAP)


def _log_sigmoid(z):
    return jnp.minimum(z, 0.0) - jnp.log1p(jnp.exp(-jnp.abs(z)))


def _mlstm_kernel(q_ref, k_ref, v_ref, op_ref, gc_ref, gr_ref, bc_ref, br_ref, ng_ref, o_ref,
                  c_ref, n_ref, m_ref, *, heads):
    h = pl.program_id(1)
    c = pl.program_id(2)
    L, dk = q_ref.shape

    @pl.when(c == 0)
    def _():
        c_ref[...] = jnp.zeros_like(c_ref)
        n_ref[...] = jnp.zeros_like(n_ref)
        m_ref[...] = jnp.zeros_like(m_ref)

    gcol = _soft_cap(gc_ref[...] + bc_ref[...])
    lane = lax.broadcasted_iota(jnp.int32, gcol.shape, 1)
    li_col = jnp.sum(jnp.where(lane == h, gcol, 0.0), axis=1, keepdims=True)
    lf_col = _log_sigmoid(jnp.sum(jnp.where(lane == h + heads, gcol, 0.0), axis=1, keepdims=True))
    grow = _soft_cap(gr_ref[...] + br_ref[...])
    sub = lax.broadcasted_iota(jnp.int32, grow.shape, 0)
    li_row = jnp.sum(jnp.where(sub == h, grow, 0.0), axis=0, keepdims=True)
    lf_row = _log_sigmoid(jnp.sum(jnp.where(sub == h + heads, grow, 0.0), axis=0, keepdims=True))

    r_i = lax.broadcasted_iota(jnp.int32, (L, L), 0)
    c_i = lax.broadcasted_iota(jnp.int32, (L, L), 1)
    causal = c_i <= r_i
    tri = jnp.where(causal, 1.0, 0.0).astype(BF16)
    triu = jnp.where(r_i <= c_i, 1.0, 0.0).astype(BF16)
    lf_col_b = jnp.broadcast_to(lf_col, (L, LANES))
    b_col = sum(jnp.dot(tri, t, preferred_element_type=F32) for t in _split3(lf_col_b))[:, :1]
    lf_row_b = jnp.broadcast_to(lf_row, (SUBLANES, L))
    b_row = sum(jnp.dot(t, triu, preferred_element_type=F32) for t in _split3(lf_row_b))[:1, :]
    g_tot = b_col[L - 1:L, :]

    m_prev = m_ref[...]
    d = jnp.where(causal, b_col - b_row + li_row, -jnp.inf)
    m_inter = b_col + m_prev
    m_t = jnp.maximum(m_inter, jnp.max(d, axis=1, keepdims=True))
    w_inter = jnp.exp(m_inter - m_t)

    q = q_ref[...] * (dk ** -0.5)
    k = k_ref[...]
    v = v_ref[...]
    ones = jnp.ones((L, LANES), BF16)
    s = lax.dot_general(q, k, (((1,), (1,)), ((), ())), preferred_element_type=F32)
    a = (jnp.exp(d - m_t) * s).astype(BF16)
    num = (w_inter * jnp.dot(q, c_ref[...].astype(BF16), preferred_element_type=F32)
           + jnp.dot(a, v, preferred_element_type=F32))
    den = (w_inter * jnp.dot(q, n_ref[...].astype(BF16), preferred_element_type=F32)[:, :1]
           + jnp.dot(a, ones, preferred_element_type=F32)[:, :1])
    hh = num / jnp.maximum(jnp.abs(den), jnp.exp(-m_t))

    w_src = g_tot - b_col + li_col
    m_new = jnp.maximum(g_tot + m_prev, jnp.max(w_src, axis=0, keepdims=True))
    decay = jnp.exp(g_tot + m_prev - m_new)
    kw = (k.astype(F32) * jnp.exp(w_src - m_new)).astype(BF16)
    contract_rows = (((0,), (0,)), ((), ()))
    c_ref[...] = decay * c_ref[...] + lax.dot_general(kw, v, contract_rows, preferred_element_type=F32)
    n_ref[...] = decay * n_ref[...] + lax.dot_general(kw, ones, contract_rows, preferred_element_type=F32)
    m_ref[...] = m_new

    hn = hh * lax.rsqrt(jnp.mean(hh * hh, axis=1, keepdims=True) + RMS_EPS)
    o_ref[...] = (jax.nn.sigmoid(op_ref[...].astype(F32)) * (hn * ng_ref[...])).astype(o_ref.dtype)


def _mlstm_recurrence(proj, gates, b_gates, norm_g, batch, seq, heads, dk, dv, chunk):
    assert seq % chunk == 0 and dv % dk == 0
    nc = seq // chunk
    kernel = functools.partial(_mlstm_kernel, heads=heads)
    v0 = 2 * heads * dk // dv
    o0 = v0 + heads
    gates_t = gates.T
    return pl.pallas_call(
        kernel,
        grid=(batch, heads, nc),
        in_specs=[pl.BlockSpec((chunk, dk), lambda b, h, c: (b * nc + c, h)),
                  pl.BlockSpec((chunk, dk), lambda b, h, c: (b * nc + c, heads + h)),
                  pl.BlockSpec((chunk, dv), lambda b, h, c: (b * nc + c, v0 + h)),
                  pl.BlockSpec((chunk, dv), lambda b, h, c: (b * nc + c, o0 + h)),
                  pl.BlockSpec((chunk, 2 * heads), lambda b, h, c: (b * nc + c, 0)),
                  pl.BlockSpec((2 * heads, chunk), lambda b, h, c: (0, b * nc + c)),
                  pl.BlockSpec((1, 2 * heads), lambda b, h, c: (0, 0)),
                  pl.BlockSpec((2 * heads, 1), lambda b, h, c: (0, 0)),
                  pl.BlockSpec((1, dv), lambda b, h, c: (0, h))],
        out_specs=pl.BlockSpec((chunk, dv), lambda b, h, c: (b * nc + c, h)),
        out_shape=jax.ShapeDtypeStruct((batch * seq, heads * dv), BF16),
        scratch_shapes=[pltpu.VMEM((dk, dv), F32), pltpu.VMEM((dk, LANES), F32), pltpu.VMEM((1, 1), F32)],
        compiler_params=_params(("parallel", "parallel", "arbitrary")),
        name="mlstm_recurrence",
    )(proj, proj, proj, proj, gates, gates_t, b_gates.reshape(1, -1), b_gates.reshape(-1, 1),
      norm_g.reshape(1, -1))


def _router_kernel(x_ref, w_ref, b_ref, idx_ref, p_ref, *, n_experts, topk):
    x = x_ref[...]
    w = w_ref[...]
    x_hi = x.astype(BF16)
    x_lo = (x - x_hi.astype(F32)).astype(BF16)
    w_hi = w.astype(BF16)
    w_lo = (w - w_hi.astype(F32)).astype(BF16)
    logits = (jnp.dot(x_hi, w_hi, preferred_element_type=F32) + jnp.dot(x_lo, w_hi, preferred_element_type=F32)
              + jnp.dot(x_hi, w_lo, preferred_element_type=F32)) + b_ref[...]
    lane = lax.broadcasted_iota(jnp.int32, logits.shape, 1)
    lane_f = lane.astype(F32)
    work = jnp.where(lane < n_experts, logits, -jnp.inf)
    idx_out = jnp.zeros(logits.shape, jnp.int32)
    val_out = jnp.full(logits.shape, -jnp.inf, F32)
    for kk in range(topk):
        mx = jnp.max(work, axis=1, keepdims=True)
        first = jnp.min(jnp.where(work == mx, lane_f, float(LANES)), axis=1, keepdims=True)
        first_i = first.astype(jnp.int32)
        idx_out = jnp.where(lane == kk, first_i, idx_out)
        val_out = jnp.where(lane == kk, mx, val_out)
        work = jnp.where(lane == first_i, -jnp.inf, work)
    e = jnp.exp(val_out - jnp.max(val_out, axis=1, keepdims=True))
    idx_ref[...] = idx_out
    p_ref[...] = e / jnp.sum(e, axis=1, keepdims=True)


def _router(x, w, b, n_experts, topk, tm=512):
    M, D = x.shape
    tm = _pick(M, tm)
    w_pad = jnp.zeros((D, LANES), F32).at[:, :n_experts].set(w)
    b_pad = jnp.zeros((1, LANES), F32).at[0, :n_experts].set(b)
    kernel = functools.partial(_router_kernel, n_experts=n_experts, topk=topk)
    out = pl.BlockSpec((tm, LANES), lambda i: (i, 0))
    return pl.pallas_call(
        kernel,
        grid=(M // tm,),
        in_specs=[pl.BlockSpec((tm, D), lambda i: (i, 0)),
                  pl.BlockSpec((D, LANES), lambda i: (0, 0)),
                  pl.BlockSpec((1, LANES), lambda i: (0, 0))],
        out_specs=[out, out],
        out_shape=[jax.ShapeDtypeStruct((M, LANES), jnp.int32), jax.ShapeDtypeStruct((M, LANES), F32)],
        compiler_params=_params(("parallel",)),
        name="moe_router",
    )(x, w_pad, b_pad)


def _start_row_gather(src_hbm, idx_ref, n_rows, dst, sem, unroll=8):
    def body(r, carry):
        pltpu.make_async_copy(src_hbm.at[pl.ds(idx_ref[0, 0, r], 1), :], dst.at[pl.ds(r, 1), :], sem).start()
        return carry
    lax.fori_loop(0, n_rows, body, 0, unroll=unroll)


def _wait_row_gather(src_hbm, dst, sem):
    pltpu.make_async_copy(src_hbm.at[pl.ds(0, dst.shape[0]), :], dst, sem).wait()


WEIGHT_CAST_ROWS = 512


def _expert_kernel(te_ref, nt_ref, tok_ref, tok_next_ref, xp_hbm, wgu_ref, bgu_ref, wd_hbm, bd_ref, y_ref,
                   xbuf, sem, wgu_bf, wd_f32, wd_bf, wd_sem, *, ff, layer):
    t = pl.program_id(0)
    n_tiles = nt_ref[0]
    tm, half = xbuf.shape[1], xbuf.shape[2]
    slot = lax.rem(t, 2)
    expert = te_ref[t]
    live = t < n_tiles
    fresh = jnp.logical_and(live, jnp.logical_or(t == 0, expert != te_ref[jnp.maximum(t - 1, 0)]))
    wd_copy = pltpu.make_async_copy(wd_hbm.at[layer, expert], wd_f32, wd_sem)

    @pl.when(t == 0)
    def _():
        _start_row_gather(xp_hbm, tok_ref, tm, xbuf.at[0], sem.at[0])

    @pl.when(t + 1 < n_tiles)
    def _():
        _start_row_gather(xp_hbm, tok_next_ref, tm, xbuf.at[1 - slot], sem.at[1 - slot])

    @pl.when(fresh)
    def _():
        wd_copy.start()

        def cast_rows(i, carry):
            rows = pl.ds(pl.multiple_of(i * WEIGHT_CAST_ROWS, WEIGHT_CAST_ROWS), WEIGHT_CAST_ROWS)
            wgu_bf[rows, :] = wgu_ref[0, rows, :].astype(BF16)
            return carry
        lax.fori_loop(0, wgu_bf.shape[0] // WEIGHT_CAST_ROWS, cast_rows, 0)

    @pl.when(live)
    def _():
        _wait_row_gather(xp_hbm, xbuf.at[slot], sem.at[slot])
        x_lo, x_hi = _unpack_bf16_pair(xbuf[slot])
        gu = (jnp.dot(x_lo.astype(BF16), wgu_bf[:half, :], preferred_element_type=F32)
              + jnp.dot(x_hi.astype(BF16), wgu_bf[half:, :], preferred_element_type=F32) + bgu_ref[0])
        gate = jnp.minimum(gu[:, :ff], SWIGLU_LIMIT)
        up = jnp.clip(gu[:, ff:], -SWIGLU_LIMIT, SWIGLU_LIMIT)
        act = ((up + 1.0) * gate * jax.nn.sigmoid(SWIGLU_ALPHA * gate)).astype(BF16)

        @pl.when(fresh)
        def _():
            wd_copy.wait()
            wd_bf[...] = wd_f32[...].astype(BF16)

        y = jnp.dot(act, wd_bf[...], preferred_element_type=F32) + bd_ref[0]
        y_ref[...] = _pack_bf16_pair(y[:, :half], y[:, half:])

    @pl.when(jnp.logical_not(live))
    def _():
        y_ref[...] = jnp.zeros_like(y_ref)


def _expert_ffn(x_packed, tile_expert, n_tiles, tok_rows, w_gate_up, b_gate_up, w_down, b_down, layer, tm):
    N, half = x_packed.shape
    D = 2 * half
    n_layers, E, _, two_ff = w_gate_up.shape
    ff = two_ff // 2
    n_tile_max = tile_expert.shape[0]
    assert N >= tm and D % WEIGHT_CAST_ROWS == 0
    kernel = functools.partial(_expert_kernel, ff=ff, layer=layer)
    tok_tiles = tok_rows.reshape(n_tile_max, 1, tm)
    grid_spec = pltpu.PrefetchScalarGridSpec(
        num_scalar_prefetch=2,
        grid=(n_tile_max,),
        in_specs=[pl.BlockSpec((1, 1, tm), lambda t, te, nt: (t, 0, 0), memory_space=pltpu.SMEM),
                  pl.BlockSpec((1, 1, tm), lambda t, te, nt: (jnp.minimum(t + 1, n_tile_max - 1), 0, 0),
                               memory_space=pltpu.SMEM),
                  pl.BlockSpec(memory_space=pl.ANY),
                  pl.BlockSpec((None, 1, D, two_ff), lambda t, te, nt: (layer, te[t], 0, 0)),
                  pl.BlockSpec((None, 1, 1, two_ff), lambda t, te, nt: (layer, te[t], 0, 0)),
                  pl.BlockSpec(memory_space=pl.ANY),
                  pl.BlockSpec((None, 1, 1, D), lambda t, te, nt: (layer, te[t], 0, 0))],
        out_specs=pl.BlockSpec((tm, half), lambda t, te, nt: (t, 0)),
        scratch_shapes=[pltpu.VMEM((2, tm, half), jnp.uint32), pltpu.SemaphoreType.DMA((2,)),
                        pltpu.VMEM((D, two_ff), BF16), pltpu.VMEM((ff, D), F32), pltpu.VMEM((ff, D), BF16),
                        pltpu.SemaphoreType.DMA(())],
    )
    return pl.pallas_call(
        kernel,
        grid_spec=grid_spec,
        out_shape=jax.ShapeDtypeStruct((n_tile_max * tm, half), jnp.uint32),
        compiler_params=_params(("arbitrary",)),
        name="moe_expert_ffn",
    )(tile_expert, n_tiles, tok_tiles, tok_tiles, x_packed,
      w_gate_up, b_gate_up.reshape(n_layers, E, 1, two_ff), w_down, b_down.reshape(n_layers, E, 1, D))


def _combine_kernel(pos_ref, pos_next_ref, p_ref, x_ref, y_hbm, g_ref, b_ref, of_ref, ob_ref, buf, sem, *, topk):
    i = pl.program_id(0)
    tc = x_ref.shape[0]
    slot = lax.rem(i, 2)

    @pl.when(i == 0)
    def _():
        _start_row_gather(y_hbm, pos_ref, topk * tc, buf.at[0], sem.at[0])

    @pl.when(i + 1 < pl.num_programs(0))
    def _():
        _start_row_gather(y_hbm, pos_next_ref, topk * tc, buf.at[1 - slot], sem.at[1 - slot])

    _wait_row_gather(y_hbm, buf.at[slot], sem.at[slot])
    half = buf.shape[2]
    p = p_ref[...]
    z_lo = DN_ALPHA * x_ref[:, :half]
    z_hi = DN_ALPHA * x_ref[:, half:]
    for kk in range(topk):
        y_lo, y_hi = _unpack_bf16_pair(buf[slot, kk * tc:(kk + 1) * tc, :])
        z_lo = z_lo + p[:, kk:kk + 1] * y_lo
        z_hi = z_hi + p[:, kk:kk + 1] * y_hi
    inv_d = 1.0 / (2 * half)
    mu = (jnp.sum(z_lo, axis=-1, keepdims=True) + jnp.sum(z_hi, axis=-1, keepdims=True)) * inv_d
    c_lo = z_lo - mu
    c_hi = z_hi - mu
    var = (jnp.sum(c_lo * c_lo, axis=-1, keepdims=True) + jnp.sum(c_hi * c_hi, axis=-1, keepdims=True)) * inv_d
    r = lax.rsqrt(var + LN_EPS)
    out_lo = c_lo * r * g_ref[:, :half] + b_ref[:, :half]
    out_hi = c_hi * r * g_ref[:, half:] + b_ref[:, half:]
    of_ref[:, :half] = out_lo
    of_ref[:, half:] = out_hi
    ob_ref[:, :half] = out_lo.astype(BF16)
    ob_ref[:, half:] = out_hi.astype(BF16)


def _combine_res_ln(x, y_rows, pos, top_p, g, b, topk, tc):
    N, D = x.shape
    tc = _pick(N, tc)
    n_tiles = N // tc
    assert y_rows.shape[0] >= topk * tc
    kernel = functools.partial(_combine_kernel, topk=topk)
    row = pl.BlockSpec((tc, D), lambda i: (i, 0))
    vec = pl.BlockSpec((1, D), lambda i: (0, 0))
    pos_tiles = pos.reshape(n_tiles, tc, topk).transpose(0, 2, 1).reshape(n_tiles, 1, topk * tc)
    return pl.pallas_call(
        kernel,
        grid=(n_tiles,),
        in_specs=[pl.BlockSpec((1, 1, tc * topk), lambda i: (i, 0, 0), memory_space=pltpu.SMEM),
                  pl.BlockSpec((1, 1, tc * topk), lambda i: (jnp.minimum(i + 1, n_tiles - 1), 0, 0),
                               memory_space=pltpu.SMEM),
                  pl.BlockSpec((tc, LANES), lambda i: (i, 0)),
                  row,
                  pl.BlockSpec(memory_space=pl.ANY),
                  vec, vec],
        out_specs=[row, row],
        out_shape=[jax.ShapeDtypeStruct((N, D), F32), jax.ShapeDtypeStruct((N, D), BF16)],
        scratch_shapes=[pltpu.VMEM((2, topk * tc, D // 2), jnp.uint32), pltpu.SemaphoreType.DMA((2,))],
        compiler_params=_params(("arbitrary",)),
        name="moe_combine_layernorm",
    )(pos_tiles, pos_tiles, top_p, x, y_rows, g.reshape(1, D), b.reshape(1, D))


def _routing_tables(top_idx, n_experts, tm):
    N, K = top_idx.shape
    n_assign = N * K
    n_tile_max = n_assign // tm + n_experts
    e_flat = top_idx.reshape(-1)
    onehot = (e_flat[:, None] == jnp.arange(n_experts, dtype=jnp.int32)[None, :]).astype(jnp.int32)
    running = jnp.cumsum(onehot, axis=0)
    sizes = running[-1]
    rank = jnp.sum(running * onehot, axis=1) - 1
    padded = (sizes + tm - 1) // tm * tm
    pstart = jnp.cumsum(padded) - padded
    pos = pstart[e_flat] + rank
    n_tiles = (jnp.sum(padded) // tm).astype(jnp.int32).reshape(1)
    tile_start = jnp.arange(n_tile_max, dtype=jnp.int32) * tm
    pend = pstart + padded
    tile_expert = jnp.minimum(jnp.sum((tile_start[:, None] >= pend[None, :]).astype(jnp.int32), axis=1),
                              n_experts - 1).astype(jnp.int32)
    tok_rows = jnp.zeros((n_tile_max * tm,), jnp.int32).at[pos].set(
        jnp.arange(n_assign, dtype=jnp.int32) // K)
    return tile_expert, n_tiles, tok_rows, pos.reshape(N, K).astype(jnp.int32)


def _moe_layer(x_f32, x_packed, router_w, router_b, w_gate_up, b_gate_up, w_down, b_down, layer, ln_g, ln_b,
               topk=MOE_TOPK, tm=MOE_ROW_TILE, tc=COMBINE_ROW_TILE):
    n_experts = router_w.shape[1]
    top_idx, top_p = _router(x_f32, router_w, router_b, n_experts, topk)
    tile_expert, n_tiles, tok_rows, pos = _routing_tables(top_idx[:, :topk], n_experts, tm)
    y_rows = _expert_ffn(x_packed, tile_expert, n_tiles, tok_rows, w_gate_up, b_gate_up, w_down, b_down, layer, tm)
    return _combine_res_ln(x_f32, y_rows, pos, top_p, ln_g, ln_b, topk, tc)


def _moba_mixer(x_bf16, w_in, w_out, layer, batch, seq, heads=MOBA_HEADS, hd=MOBA_HEAD_DIM, bs=MOBA_BLOCK,
                topk=MOBA_TOPK):
    qkv = _matmul(x_bf16, w_in, layer, BF16)
    o = _moba_attention(qkv, batch, seq, heads, hd, bs, topk)
    return _matmul(o, w_out, layer, F32)


def _mlstm_mixer(x_bf16, w_in, b_gates, norm_g, w_out, layer, batch, seq, heads=MLSTM_HEADS, dk=MLSTM_QK_DIM,
                 dv=MLSTM_V_DIM, chunk=MLSTM_KERNEL_CHUNK):
    n_main = 2 * heads * (dk + dv)
    proj = _matmul(x_bf16, w_in, layer, BF16, n_cols=n_main)
    w_gates = jnp.zeros((1, w_in.shape[1], LANES), F32).at[0, :, :2 * heads].set(w_in[layer, :, n_main:])
    gates = _matmul(x_bf16, w_gates, 0, F32)[:, :2 * heads]
    hg = _mlstm_recurrence(proj, gates, b_gates, norm_g, batch, seq, heads, dk, dv, min(chunk, seq))
    return _matmul(hg, w_out, layer, F32)


def kernel(x, moba_w_in, moba_w_out, mlstm_w_in, mlstm_b_gates, mlstm_norm_g, mlstm_w_out, ln_g, ln_b,
           router_w, router_b, moe_w_gate_up, moe_b_gate_up, moe_w_down, moe_b_down):
    batch, seq, d = x.shape
    x_f32 = x.reshape(batch * seq, d)
    x_bf16 = x_f32.astype(BF16)
    for i in range(DEPTH):
        slot = i // 2
        if i % 2 == 0:
            y = _moba_mixer(x_bf16, moba_w_in, moba_w_out, slot, batch, seq)
        else:
            y = _mlstm_mixer(x_bf16, mlstm_w_in, mlstm_b_gates[slot], mlstm_norm_g[slot], mlstm_w_out, slot,
                             batch, seq)
        x_f32, x_packed = _res_ln(x_f32, y, ln_g[i, 0], ln_b[i, 0])
        x_f32, x_bf16 = _moe_layer(x_f32, x_packed, router_w[i], router_b[i], moe_w_gate_up, moe_b_gate_up,
                                   moe_w_down, moe_b_down, i, ln_g[i, 1], ln_b[i, 1])
    return x_f32.reshape(batch, seq, d)
```

```python
import functools

import jax
import jax.numpy as jnp
from jax import lax
from jax.experimental import pallas as pl
from jax.experimental.pallas import tpu as pltpu

F32 = jnp.float32
BF16 = jnp.bfloat16

D_MODEL = 4096
DEPTH = 4
MOBA_HEADS = 32
MOBA_HEAD_DIM = D_MODEL // MOBA_HEADS
MOBA_BLOCK = 256
MOBA_TOPK = 3
MLSTM_HEADS = 8
MLSTM_QK_DIM = D_MODEL // 2 // MLSTM_HEADS
MLSTM_V_DIM = D_MODEL // MLSTM_HEADS
MLSTM_GATE_CAP = 15.0
N_EXPERTS = 32
MOE_TOPK = 4
EXPERT_FF = 384
SWIGLU_LIMIT = 7.0
SWIGLU_ALPHA = 1.702
DN_ALPHA = (2 * DEPTH) ** 0.25
LN_EPS = 1e-5
RMS_EPS = 1e-6
LOG2_E = 1.4426950408889634

LANES = 128
SUBLANES = 8
VMEM_LIMIT_BYTES = 56 * 1024 * 1024

MLSTM_KERNEL_CHUNK = 256
MOE_ROW_TILE = 256
COMBINE_ROW_TILE = 128


def _params(semantics):
    return pltpu.CompilerParams(dimension_semantics=semantics, vmem_limit_bytes=VMEM_LIMIT_BYTES)


def _pick(n, target):
    t = min(n, target)
    while n % t:
        t //= 2
    return t


def _matmul_kernel(a_ref, b_ref, o_ref):
    o_ref[...] = jnp.dot(a_ref[...], b_ref[...].astype(BF16), preferred_element_type=F32).astype(o_ref.dtype)


def _matmul(a, b, layer, out_dtype, tm=2048, tn=512):
    M, K = a.shape
    N = b.shape[2]
    tm, tn = _pick(M, tm), _pick(N, tn)
    return pl.pallas_call(
        _matmul_kernel,
        grid=(M // tm, N // tn),
        in_specs=[pl.BlockSpec((tm, K), lambda i, j: (i, 0), pipeline_mode=pl.Buffered(1)),
                  pl.BlockSpec((None, K, tn), lambda i, j: (layer, 0, j))],
        out_specs=pl.BlockSpec((tm, tn), lambda i, j: (i, j)),
        out_shape=jax.ShapeDtypeStruct((M, N), out_dtype),
        compiler_params=_params(("parallel", "parallel")),
        name="dense_matmul",
    )(a, b)


def _layer_norm_rows(z, g, b):
    mu = jnp.mean(z, axis=-1, keepdims=True)
    zc = z - mu
    var = jnp.mean(zc * zc, axis=-1, keepdims=True)
    return zc * lax.rsqrt(var + LN_EPS) * g + b


HIGH_HALF = 0xFFFF0000


def _pack_bf16_pair(lo, hi):
    lo_bits = lax.bitcast_convert_type(lo.astype(BF16).astype(F32), jnp.uint32)
    hi_bits = lax.bitcast_convert_type(hi.astype(BF16).astype(F32), jnp.uint32)
    return (hi_bits & jnp.uint32(HIGH_HALF)) | (lo_bits >> 16)


def _unpack_bf16_pair(w):
    lo = lax.bitcast_convert_type(w << 16, F32)
    hi = lax.bitcast_convert_type(w & jnp.uint32(HIGH_HALF), F32)
    return lo, hi


def _slab_pitch(chunks):
    pitch = -(-chunks // 4) * 4
    return pitch + 4 if pitch % 8 == 0 else pitch


def _store_slabs(ref, packed):
    n, half = packed.shape
    chunks = half // LANES
    pitch = ref.shape[0] // n
    for c in range(chunks):
        ref[pl.ds(c, n, stride=pitch), :] = packed[:, c * LANES:(c + 1) * LANES]
    for c in range(chunks, pitch):
        ref[pl.ds(c, n, stride=pitch), :] = jnp.zeros((n, LANES), packed.dtype)


def _load_slabs(ref, slot, first, n, chunks, pitch):
    return jnp.concatenate(
        [ref[slot, pl.ds(first * pitch + c, n, stride=pitch), :] for c in range(chunks)], axis=1)


def _route_rows(x, w, b, n_experts, topk):
    x_hi = x.astype(BF16)
    x_lo = (x - x_hi.astype(F32)).astype(BF16)
    w_hi = w.astype(BF16)
    w_lo = (w - w_hi.astype(F32)).astype(BF16)
    logits = (jnp.dot(x_hi, w_hi, preferred_element_type=F32) + jnp.dot(x_lo, w_hi, preferred_element_type=F32)
              + jnp.dot(x_hi, w_lo, preferred_element_type=F32)) + b
    lane = lax.broadcasted_iota(jnp.int32, logits.shape, 1)
    lane_f = lane.astype(F32)
    work = jnp.where(lane < n_experts, logits, -jnp.inf)
    idx_out = jnp.zeros(logits.shape, jnp.int32)
    val_out = jnp.full(logits.shape, -jnp.inf, F32)
    for kk in range(topk):
        mx = jnp.max(work, axis=1, keepdims=True)
        first = jnp.min(jnp.where(work == mx, lane_f, float(LANES)), axis=1, keepdims=True)
        first_i = first.astype(jnp.int32)
        idx_out = jnp.where(lane == kk, first_i, idx_out)
        val_out = jnp.where(lane == kk, mx, val_out)
        work = jnp.where(lane == first_i, -jnp.inf, work)
    e = jnp.exp(val_out - jnp.max(val_out, axis=1, keepdims=True))
    return idx_out, e / jnp.sum(e, axis=1, keepdims=True)


def _res_ln_route_kernel(x_ref, y_ref, g_ref, b_ref, rw_ref, rb_ref, of_ref, slab_ref, idx_ref, p_ref,
                         *, n_experts, topk):
    half = x_ref.shape[1] // 2
    z = DN_ALPHA * x_ref[...] + y_ref[...].astype(F32)
    out = _layer_norm_rows(z, g_ref[...], b_ref[...])
    of_ref[...] = out
    _store_slabs(slab_ref, _pack_bf16_pair(out[:, :half], out[:, half:]))
    idx_ref[...], p_ref[...] = _route_rows(out, rw_ref[...], rb_ref[...], n_experts, topk)


def _res_ln_route(x, y, g, b, router_w, router_b, topk, tm=256):
    M, D = x.shape
    n_experts = router_w.shape[1]
    tm = _pick(M, tm)
    pitch = _slab_pitch(D // 2 // LANES)
    w_pad = jnp.zeros((D, LANES), F32).at[:, :n_experts].set(router_w)
    b_pad = jnp.zeros((1, LANES), F32).at[0, :n_experts].set(router_b)
    row = pl.BlockSpec((tm, D), lambda i: (i, 0))
    vec = pl.BlockSpec((1, D), lambda i: (0, 0))
    lanes = pl.BlockSpec((tm, LANES), lambda i: (i, 0))
    kernel = functools.partial(_res_ln_route_kernel, n_experts=n_experts, topk=topk)
    return pl.pallas_call(
        kernel,
        grid=(M // tm,),
        in_specs=[row, row, vec, vec, pl.BlockSpec((D, LANES), lambda i: (0, 0)),
                  pl.BlockSpec((1, LANES), lambda i: (0, 0))],
        out_specs=[row, pl.BlockSpec((tm * pitch, LANES), lambda i: (i, 0)), lanes, lanes],
        out_shape=[jax.ShapeDtypeStruct((M, D), F32), jax.ShapeDtypeStruct((M * pitch, LANES), jnp.uint32),
                   jax.ShapeDtypeStruct((M, LANES), jnp.int32), jax.ShapeDtypeStruct((M, LANES), F32)],
        compiler_params=_params(("parallel",)),
        name="residual_layernorm_route",
    )(x, y, g.reshape(1, D), b.reshape(1, D), w_pad, b_pad)


def _moba_kernel(q_ref, k_ref, v_ref, o_ref, kmean_ref, vt_ref, s_ref, p_ref, *, nb, bs, topk, scale):
    c = scale * LOG2_E
    nt = (((1,), (1,)), ((), ()))
    neg_inf = -jnp.inf

    kmean_ref[...] = jnp.zeros_like(kmean_ref)
    for n in range(nb):
        blk = slice(n * bs, (n + 1) * bs)
        kmean_ref[n:n + 1, :] = jnp.mean(k_ref[blk, :].astype(F32), axis=0, keepdims=True)
        vt_ref[:, blk] = v_ref[blk, :].T

    kmean = kmean_ref[...]
    km_hi = kmean.astype(BF16)
    km_lo = (kmean - km_hi.astype(F32)).astype(BF16)
    q_all = q_ref[...]
    gate = (lax.dot_general(km_hi, q_all, nt, preferred_element_type=F32)
            + lax.dot_general(km_lo, q_all, nt, preferred_element_type=F32))

    def scores(j):
        s_ref[j % 2, 0:(j + 1) * bs, :] = lax.dot_general(
            k_ref[0:(j + 1) * bs, :], q_ref[j * bs:(j + 1) * bs, :], nt, preferred_element_type=F32)

    def softmax(j):
        slot = j % 2
        own = slice(j * bs, (j + 1) * bs)
        key_i = lax.broadcasted_iota(jnp.int32, (bs, bs), 0)
        qry_i = lax.broadcasted_iota(jnp.int32, (bs, bs), 1)
        causal = key_i <= qry_i
        m = jnp.max(jnp.where(causal, s_ref[slot, own, :], neg_inf), axis=0, keepdims=True)
        sel = [None] * j
        if j > topk:
            g = gate[:, own]
            blk_i = lax.broadcasted_iota(jnp.int32, g.shape, 0)
            g = jnp.where(blk_i < j, g, neg_inf)
        for n in range(j):
            bm = jnp.max(s_ref[slot, n * bs:(n + 1) * bs, :], axis=0, keepdims=True)
            if j > topk:
                gn = g[n:n + 1, :]
                beats = (g > gn) | ((g == gn) & (blk_i < n))
                sel[n] = jnp.sum(beats.astype(F32), axis=0, keepdims=True) < float(topk)
                bm = jnp.where(sel[n], bm, neg_inf)
            m = jnp.maximum(m, bm)
        p = jnp.exp2((jnp.where(causal, s_ref[slot, own, :], neg_inf) - m) * c)
        l = jnp.sum(p, axis=0, keepdims=True)
        p_ref[slot, own, :] = p.astype(BF16)
        for n in range(j):
            blk = slice(n * bs, (n + 1) * bs)
            m_use = m if sel[n] is None else jnp.where(sel[n], m, jnp.inf)
            p = jnp.exp2((s_ref[slot, blk, :] - m_use) * c)
            l = l + jnp.sum(p, axis=0, keepdims=True)
            p_ref[slot, blk, :] = p.astype(BF16)
        return l

    scores(0)
    for j in range(nb):
        if j + 1 < nb:
            scores(j + 1)
        l = softmax(j)
        nk = (j + 1) * bs
        acc = jnp.dot(vt_ref[:, 0:nk], p_ref[j % 2, 0:nk, :], preferred_element_type=F32)
        o_ref[j * bs:(j + 1) * bs, :] = (acc * (1.0 / l)).T.astype(o_ref.dtype)


def _moba_attention(qkv, batch, seq, heads, hd, bs, topk):
    assert seq % bs == 0 and hd % LANES == 0 and bs % LANES == 0
    nb = seq // bs
    nb_pad = -(-nb // SUBLANES) * SUBLANES
    kernel = functools.partial(_moba_kernel, nb=nb, bs=bs, topk=min(topk, nb), scale=hd ** -0.5)
    return pl.pallas_call(
        kernel,
        grid=(batch, heads),
        in_specs=[pl.BlockSpec((seq, hd), lambda b, h: (b, h)),
                  pl.BlockSpec((seq, hd), lambda b, h: (b, heads + h)),
                  pl.BlockSpec((seq, hd), lambda b, h: (b, 2 * heads + h))],
        out_specs=pl.BlockSpec((seq, hd), lambda b, h: (b, h)),
        out_shape=jax.ShapeDtypeStruct((batch * seq, heads * hd), BF16),
        scratch_shapes=[pltpu.VMEM((nb_pad, hd), F32), pltpu.VMEM((hd, seq), BF16),
                        pltpu.VMEM((2, seq, bs), F32), pltpu.VMEM((2, seq, bs), BF16)],
        compiler_params=_params(("parallel", "parallel")),
        name="moba_attention",
    )(qkv, qkv, qkv)


def _split3(x):
    h1 = x.astype(BF16)
    r1 = x - h1.astype(F32)
    h2 = r1.astype(BF16)
    r2 = r1 - h2.astype(F32)
    return h1, h2, r2.astype(BF16)


def _soft_cap(z):
    return MLSTM_GATE_CAP * jnp.tanh(z / MLSTM_GATE_CAP)


def _log_sigmoid(z):
    return jnp.minimum(z, 0.0) - jnp.log1p(jnp.exp(-jnp.abs(z)))


def _mlstm_kernel(q_ref, k_ref, v_ref, op_ref, gc_ref, gr_ref, bc_ref, br_ref, ng_ref, o_ref,
                  c_ref, n_ref, m_ref, *, heads):
    h = pl.program_id(1)
    c = pl.program_id(2)
    L, dk = q_ref.shape

    @pl.when(c == 0)
    def _():
        c_ref[...] = jnp.zeros_like(c_ref)
        n_ref[...] = jnp.zeros_like(n_ref)
        m_ref[...] = jnp.zeros_like(m_ref)

    gcol = _soft_cap(gc_ref[...] + bc_ref[...])
    lane = lax.broadcasted_iota(jnp.int32, gcol.shape, 1)
    li_col = jnp.sum(jnp.where(lane == h, gcol, 0.0), axis=1, keepdims=True)
    lf_col = _log_sigmoid(jnp.sum(jnp.where(lane == h + heads, gcol, 0.0), axis=1, keepdims=True))
    grow = _soft_cap(gr_ref[...] + br_ref[...])
    sub = lax.broadcasted_iota(jnp.int32, grow.shape, 0)
    li_row = jnp.sum(jnp.where(sub == h, grow, 0.0), axis=0, keepdims=True)
    lf_row = _log_sigmoid(jnp.sum(jnp.where(sub == h + heads, grow, 0.0), axis=0, keepdims=True))

    r_i = lax.broadcasted_iota(jnp.int32, (L, L), 0)
    c_i = lax.broadcasted_iota(jnp.int32, (L, L), 1)
    causal = c_i <= r_i
    tri = jnp.where(causal, 1.0, 0.0).astype(BF16)
    triu = jnp.where(r_i <= c_i, 1.0, 0.0).astype(BF16)
    lf_col_b = jnp.broadcast_to(lf_col, (L, LANES))
    b_col = sum(jnp.dot(tri, t, preferred_element_type=F32) for t in _split3(lf_col_b))[:, :1]
    lf_row_b = jnp.broadcast_to(lf_row, (SUBLANES, L))
    b_row = sum(jnp.dot(t, triu, preferred_element_type=F32) for t in _split3(lf_row_b))[:1, :]
    g_tot = b_col[L - 1:L, :]

    m_prev = m_ref[...]
    d = jnp.where(causal, b_col - b_row + li_row, -jnp.inf)
    m_inter = b_col + m_prev
    m_t = jnp.maximum(m_inter, jnp.max(d, axis=1, keepdims=True))
    w_inter = jnp.exp(m_inter - m_t)

    q = q_ref[...] * (dk ** -0.5)
    k = k_ref[...]
    v = v_ref[...]
    ones = jnp.ones((L, LANES), BF16)
    s = lax.dot_general(q, k, (((1,), (1,)), ((), ())), preferred_element_type=F32)
    a = (jnp.exp(d - m_t) * s).astype(BF16)
    num = (w_inter * jnp.dot(q, c_ref[...].astype(BF16), preferred_element_type=F32)
           + jnp.dot(a, v, preferred_element_type=F32))
    den = (w_inter * jnp.dot(q, n_ref[...].astype(BF16), preferred_element_type=F32)[:, :1]
           + jnp.dot(a, ones, preferred_element_type=F32)[:, :1])
    hh = num / jnp.maximum(jnp.abs(den), jnp.exp(-m_t))

    w_src = g_tot - b_col + li_col
    m_new = jnp.maximum(g_tot + m_prev, jnp.max(w_src, axis=0, keepdims=True))
    decay = jnp.exp(g_tot + m_prev - m_new)
    kw = (k.astype(F32) * jnp.exp(w_src - m_new)).astype(BF16)
    contract_rows = (((0,), (0,)), ((), ()))
    c_ref[...] = decay * c_ref[...] + lax.dot_general(kw, v, contract_rows, preferred_element_type=F32)
    n_ref[...] = decay * n_ref[...] + lax.dot_general(kw, ones, contract_rows, preferred_element_type=F32)
    m_ref[...] = m_new

    hn = hh * lax.rsqrt(jnp.mean(hh * hh, axis=1, keepdims=True) + RMS_EPS)
    o_ref[...] = (jax.nn.sigmoid(op_ref[...].astype(F32)) * (hn * ng_ref[...])).astype(o_ref.dtype)


def _mlstm_recurrence(proj, gates, b_gates, norm_g, batch, seq, heads, dk, dv, chunk):
    assert seq % chunk == 0 and dv % dk == 0
    nc = seq // chunk
    kernel = functools.partial(_mlstm_kernel, heads=heads)
    v0 = 2 * heads * dk // dv
    o0 = v0 + heads
    gates_t = gates.T
    return pl.pallas_call(
        kernel,
        grid=(batch, heads, nc),
        in_specs=[pl.BlockSpec((chunk, dk), lambda b, h, c: (b * nc + c, h)),
                  pl.BlockSpec((chunk, dk), lambda b, h, c: (b * nc + c, heads + h)),
                  pl.BlockSpec((chunk, dv), lambda b, h, c: (b * nc + c, v0 + h)),
                  pl.BlockSpec((chunk, dv), lambda b, h, c: (b * nc + c, o0 + h)),
                  pl.BlockSpec((chunk, 2 * heads), lambda b, h, c: (b * nc + c, 0)),
                  pl.BlockSpec((2 * heads, chunk), lambda b, h, c: (0, b * nc + c)),
                  pl.BlockSpec((1, 2 * heads), lambda b, h, c: (0, 0)),
                  pl.BlockSpec((2 * heads, 1), lambda b, h, c: (0, 0)),
                  pl.BlockSpec((1, dv), lambda b, h, c: (0, h))],
        out_specs=pl.BlockSpec((chunk, dv), lambda b, h, c: (b * nc + c, h)),
        out_shape=jax.ShapeDtypeStruct((batch * seq, heads * dv), BF16),
        scratch_shapes=[pltpu.VMEM((dk, dv), F32), pltpu.VMEM((dk, LANES), F32), pltpu.VMEM((1, 1), F32)],
        compiler_params=_params(("parallel", "parallel", "arbitrary")),
        name="mlstm_recurrence",
    )(proj, proj, proj, proj, gates, gates_t, b_gates.reshape(1, -1), b_gates.reshape(-1, 1),
      norm_g.reshape(1, -1))


def _start_slab_gather(src_hbm, row_ref, n_slabs, pitch, dst, sem):
    for r in range(n_slabs):
        row0 = pl.multiple_of(row_ref[0, 0, r], 4)
        pltpu.make_async_copy(src_hbm.at[pl.ds(row0, pitch), :], dst.at[pl.ds(r * pitch, pitch), :], sem).start()


def _wait_slab_gather(src_hbm, dst, sem):
    pltpu.make_async_copy(src_hbm.at[pl.ds(0, dst.shape[0]), :], dst, sem).wait()


WEIGHT_CAST_ROWS = 512


def _expert_kernel(te_ref, nt_ref, tok_ref, tok_next_ref, xp_hbm, wgu_ref, bgu_ref, wd_hbm, bd_ref, y_ref,
                   xbuf, sem, wgu_bf, wd_f32, wd_bf, wd_sem, *, ff, layer):
    t = pl.program_id(0)
    n_tiles = nt_ref[0]
    half = wd_bf.shape[1] // 2
    chunks = half // LANES
    pitch = _slab_pitch(chunks)
    tm = xbuf.shape[1] // pitch
    slot = lax.rem(t, 2)
    expert = te_ref[t]
    live = t < n_tiles
    fresh = jnp.logical_and(live, jnp.logical_or(t == 0, expert != te_ref[jnp.maximum(t - 1, 0)]))
    wd_copy = pltpu.make_async_copy(wd_hbm.at[layer, expert], wd_f32, wd_sem)

    @pl.when(t == 0)
    def _():
        _start_slab_gather(xp_hbm, tok_ref, tm, pitch, xbuf.at[0], sem.at[0])

    @pl.when(t + 1 < n_tiles)
    def _():
        _start_slab_gather(xp_hbm, tok_next_ref, tm, pitch, xbuf.at[1 - slot], sem.at[1 - slot])

    @pl.when(fresh)
    def _():
        wd_copy.start()

        def cast_rows(i, carry):
            rows = pl.ds(pl.multiple_of(i * WEIGHT_CAST_ROWS, WEIGHT_CAST_ROWS), WEIGHT_CAST_ROWS)
            wgu_bf[rows, :] = wgu_ref[0, rows, :].astype(BF16)
            return carry
        lax.fori_loop(0, wgu_bf.shape[0] // WEIGHT_CAST_ROWS, cast_rows, 0)

    @pl.when(live)
    def _():
        _wait_slab_gather(xp_hbm, xbuf.at[slot], sem.at[slot])
        x_lo, x_hi = _unpack_bf16_pair(_load_slabs(xbuf, slot, 0, tm, chunks, pitch))
        gu = (jnp.dot(x_lo.astype(BF16), wgu_bf[:half, :], preferred_element_type=F32)
              + jnp.dot(x_hi.astype(BF16), wgu_bf[half:, :], preferred_element_type=F32) + bgu_ref[0])
        gate = jnp.minimum(gu[:, :ff], SWIGLU_LIMIT)
        up = jnp.clip(gu[:, ff:], -SWIGLU_LIMIT, SWIGLU_LIMIT)
        act = ((up + 1.0) * gate * jax.nn.sigmoid(SWIGLU_ALPHA * gate)).astype(BF16)

        @pl.when(fresh)
        def _():
            wd_copy.wait()
            wd_bf[...] = wd_f32[...].astype(BF16)

        y = jnp.dot(act, wd_bf[...], preferred_element_type=F32) + bd_ref[0]
        _store_slabs(y_ref, _pack_bf16_pair(y[:, :half], y[:, half:]))

    @pl.when(jnp.logical_not(live))
    def _():
        y_ref[...] = jnp.zeros_like(y_ref)


def _expert_ffn(x_slabs, tile_expert, n_tiles, tok_rows, w_gate_up, b_gate_up, w_down, b_down, layer, tm):
    n_layers, E, D, two_ff = w_gate_up.shape
    half = D // 2
    pitch = _slab_pitch(half // LANES)
    N = x_slabs.shape[0] // pitch
    ff = two_ff // 2
    n_tile_max = tile_expert.shape[0]
    assert N >= tm and D % WEIGHT_CAST_ROWS == 0
    kernel = functools.partial(_expert_kernel, ff=ff, layer=layer)
    tok_tiles = (tok_rows * pitch).reshape(n_tile_max, 1, tm)
    grid_spec = pltpu.PrefetchScalarGridSpec(
        num_scalar_prefetch=2,
        grid=(n_tile_max,),
        in_specs=[pl.BlockSpec((1, 1, tm), lambda t, te, nt: (t, 0, 0), memory_space=pltpu.SMEM),
                  pl.BlockSpec((1, 1, tm), lambda t, te, nt: (jnp.minimum(t + 1, n_tile_max - 1), 0, 0),
                               memory_space=pltpu.SMEM),
                  pl.BlockSpec(memory_space=pl.ANY),
                  pl.BlockSpec((None, 1, D, two_ff), lambda t, te, nt: (layer, te[t], 0, 0)),
                  pl.BlockSpec((None, 1, 1, two_ff), lambda t, te, nt: (layer, te[t], 0, 0)),
                  pl.BlockSpec(memory_space=pl.ANY),
                  pl.BlockSpec((None, 1, 1, D), lambda t, te, nt: (layer, te[t], 0, 0))],
        out_specs=pl.BlockSpec((tm * pitch, LANES), lambda t, te, nt: (t, 0)),
        scratch_shapes=[pltpu.VMEM((2, tm * pitch, LANES), jnp.uint32), pltpu.SemaphoreType.DMA((2,)),
                        pltpu.VMEM((D, two_ff), BF16), pltpu.VMEM((ff, D), F32), pltpu.VMEM((ff, D), BF16),
                        pltpu.SemaphoreType.DMA(())],
    )
    return pl.pallas_call(
        kernel,
        grid_spec=grid_spec,
        out_shape=jax.ShapeDtypeStruct((n_tile_max * tm * pitch, LANES), jnp.uint32),
        compiler_params=_params(("arbitrary",)),
        name="moe_expert_ffn",
    )(tile_expert, n_tiles, tok_tiles, tok_tiles, x_slabs,
      w_gate_up, b_gate_up.reshape(n_layers, E, 1, two_ff), w_down, b_down.reshape(n_layers, E, 1, D))


def _combine_kernel(pos_ref, pos_next_ref, p_ref, x_ref, y_hbm, g_ref, b_ref, of_ref, ob_ref, buf, sem, *, topk):
    i = pl.program_id(0)
    tc, d = x_ref.shape
    half = d // 2
    chunks = half // LANES
    pitch = _slab_pitch(chunks)
    slot = lax.rem(i, 2)

    @pl.when(i == 0)
    def _():
        _start_slab_gather(y_hbm, pos_ref, topk * tc, pitch, buf.at[0], sem.at[0])

    @pl.when(i + 1 < pl.num_programs(0))
    def _():
        _start_slab_gather(y_hbm, pos_next_ref, topk * tc, pitch, buf.at[1 - slot], sem.at[1 - slot])

    _wait_slab_gather(y_hbm, buf.at[slot], sem.at[slot])
    p = p_ref[...]
    z_lo = DN_ALPHA * x_ref[:, :half]
    z_hi = DN_ALPHA * x_ref[:, half:]
    for kk in range(topk):
        y_lo, y_hi = _unpack_bf16_pair(_load_slabs(buf, slot, kk * tc, tc, chunks, pitch))
        z_lo = z_lo + p[:, kk:kk + 1] * y_lo
        z_hi = z_hi + p[:, kk:kk + 1] * y_hi
    inv_d = 1.0 / (2 * half)
    mu = (jnp.sum(z_lo, axis=-1, keepdims=True) + jnp.sum(z_hi, axis=-1, keepdims=True)) * inv_d
    c_lo = z_lo - mu
    c_hi = z_hi - mu
    var = (jnp.sum(c_lo * c_lo, axis=-1, keepdims=True) + jnp.sum(c_hi * c_hi, axis=-1, keepdims=True)) * inv_d
    r = lax.rsqrt(var + LN_EPS)
    out_lo = c_lo * r * g_ref[:, :half] + b_ref[:, :half]
    out_hi = c_hi * r * g_ref[:, half:] + b_ref[:, half:]
    of_ref[:, :half] = out_lo
    of_ref[:, half:] = out_hi
    ob_ref[:, :half] = out_lo.astype(BF16)
    ob_ref[:, half:] = out_hi.astype(BF16)


def _combine_res_ln(x, y_slabs, pos, top_p, g, b, topk, tc):
    N, D = x.shape
    tc = _pick(N, tc)
    n_tiles = N // tc
    pitch = _slab_pitch(D // 2 // LANES)
    assert y_slabs.shape[0] >= topk * tc * pitch
    kernel = functools.partial(_combine_kernel, topk=topk)
    row = pl.BlockSpec((tc, D), lambda i: (i, 0))
    vec = pl.BlockSpec((1, D), lambda i: (0, 0))
    pos_tiles = (pos * pitch).reshape(n_tiles, tc, topk).transpose(0, 2, 1).reshape(n_tiles, 1, topk * tc)
    return pl.pallas_call(
        kernel,
        grid=(n_tiles,),
        in_specs=[pl.BlockSpec((1, 1, tc * topk), lambda i: (i, 0, 0), memory_space=pltpu.SMEM),
                  pl.BlockSpec((1, 1, tc * topk), lambda i: (jnp.minimum(i + 1, n_tiles - 1), 0, 0),
                               memory_space=pltpu.SMEM),
                  pl.BlockSpec((tc, LANES), lambda i: (i, 0)),
                  row,
                  pl.BlockSpec(memory_space=pl.ANY),
                  vec, vec],
        out_specs=[row, row],
        out_shape=[jax.ShapeDtypeStruct((N, D), F32), jax.ShapeDtypeStruct((N, D), BF16)],
        scratch_shapes=[pltpu.VMEM((2, topk * tc * pitch, LANES), jnp.uint32), pltpu.SemaphoreType.DMA((2,))],
        compiler_params=_params(("arbitrary",)),
        name="moe_combine_layernorm",
    )(pos_tiles, pos_tiles, top_p, x, y_slabs, g.reshape(1, D), b.reshape(1, D))


def _routing_tables(top_idx, n_experts, tm):
    N, K = top_idx.shape
    n_assign = N * K
    n_tile_max = n_assign // tm + n_experts
    e_flat = top_idx.reshape(-1)
    onehot = (e_flat[:, None] == jnp.arange(n_experts, dtype=jnp.int32)[None, :]).astype(jnp.int32)
    running = jnp.cumsum(onehot, axis=0)
    sizes = running[-1]
    rank = jnp.sum(running * onehot, axis=1) - 1
    padded = (sizes + tm - 1) // tm * tm
    pstart = jnp.cumsum(padded) - padded
    pos = pstart[e_flat] + rank
    n_tiles = (jnp.sum(padded) // tm).astype(jnp.int32).reshape(1)
    tile_start = jnp.arange(n_tile_max, dtype=jnp.int32) * tm
    pend = pstart + padded
    tile_expert = jnp.minimum(jnp.sum((tile_start[:, None] >= pend[None, :]).astype(jnp.int32), axis=1),
                              n_experts - 1).astype(jnp.int32)
    tok_rows = jnp.zeros((n_tile_max * tm,), jnp.int32).at[pos].set(
        jnp.arange(n_assign, dtype=jnp.int32) // K)
    return tile_expert, n_tiles, tok_rows, pos.reshape(N, K).astype(jnp.int32)


def _post_mixer(x_f32, y, router_w, router_b, w_gate_up, b_gate_up, w_down, b_down, layer, ln_g, ln_b,
                topk=MOE_TOPK, tm=MOE_ROW_TILE, tc=COMBINE_ROW_TILE):
    n_experts = router_w.shape[1]
    h_f32, h_slabs, top_idx, top_p = _res_ln_route(x_f32, y, ln_g[0], ln_b[0], router_w, router_b, topk)
    tile_expert, n_tiles, tok_rows, pos = _routing_tables(top_idx[:, :topk], n_experts, tm)
    y_slabs = _expert_ffn(h_slabs, tile_expert, n_tiles, tok_rows, w_gate_up, b_gate_up, w_down, b_down, layer, tm)
    return _combine_res_ln(h_f32, y_slabs, pos, top_p, ln_g[1], ln_b[1], topk, tc)


def _moba_mixer(x_bf16, w_in, w_out, layer, batch, seq, heads=MOBA_HEADS, hd=MOBA_HEAD_DIM, bs=MOBA_BLOCK,
                topk=MOBA_TOPK):
    qkv = _matmul(x_bf16, w_in, layer, BF16)
    o = _moba_attention(qkv, batch, seq, heads, hd, bs, topk)
    return _matmul(o, w_out, layer, F32)


def _mlstm_mixer(x_bf16, w_main, w_gates, b_gates, norm_g, w_out, layer, batch, seq, heads=MLSTM_HEADS,
                 dk=MLSTM_QK_DIM, dv=MLSTM_V_DIM, chunk=MLSTM_KERNEL_CHUNK):
    proj = _matmul(x_bf16, w_main, layer, BF16)
    w_gates_pad = jnp.zeros((1, w_gates.shape[0], LANES), F32).at[0, :, :2 * heads].set(w_gates)
    gates = _matmul(x_bf16, w_gates_pad, 0, F32)[:, :2 * heads]
    hg = _mlstm_recurrence(proj, gates, b_gates, norm_g, batch, seq, heads, dk, dv, min(chunk, seq))
    return _matmul(hg, w_out, layer, F32)


def kernel(x, moba_w_in, moba_w_out, mlstm_w_in, mlstm_b_gates, mlstm_norm_g, mlstm_w_out, ln_g, ln_b,
           router_w, router_b, moe_w_gate_up, moe_b_gate_up, moe_w_down, moe_b_down):
    batch, seq, d = x.shape
    x_f32 = x.reshape(batch * seq, d)
    x_bf16 = x_f32.astype(BF16)
    n_main = 2 * MLSTM_HEADS * (MLSTM_QK_DIM + MLSTM_V_DIM)
    mlstm_w_main = mlstm_w_in[:, :, :n_main].astype(BF16)
    for i in range(DEPTH):
        slot = i // 2
        if i % 2 == 0:
            y = _moba_mixer(x_bf16, moba_w_in, moba_w_out, slot, batch, seq)
        else:
            y = _mlstm_mixer(x_bf16, mlstm_w_main, mlstm_w_in[slot, :, n_main:], mlstm_b_gates[slot],
                             mlstm_norm_g[slot], mlstm_w_out, slot, batch, seq)
        x_f32, x_bf16 = _post_mixer(x_f32, y, router_w[i], router_b[i], moe_w_gate_up, moe_b_gate_up,
                                    moe_w_down, moe_b_down, i, ln_g[i], ln_b[i])
    return x_f32.reshape(batch, seq, d)
```

```python
import functools

import jax
import jax.numpy as jnp
from jax import lax
from jax.experimental import pallas as pl
from jax.experimental.pallas import tpu as pltpu

F32 = jnp.float32
BF16 = jnp.bfloat16

D_MODEL = 4096
DEPTH = 4
MOBA_HEADS = 32
MOBA_HEAD_DIM = D_MODEL // MOBA_HEADS
MOBA_BLOCK = 256
MOBA_TOPK = 3
MLSTM_HEADS = 8
MLSTM_QK_DIM = D_MODEL // 2 // MLSTM_HEADS
MLSTM_V_DIM = D_MODEL // MLSTM_HEADS
MLSTM_GATE_CAP = 15.0
N_EXPERTS = 32
MOE_TOPK = 4
EXPERT_FF = 384
SWIGLU_LIMIT = 7.0
SWIGLU_ALPHA = 1.702
DN_ALPHA = (2 * DEPTH) ** 0.25
LN_EPS = 1e-5
RMS_EPS = 1e-6
LOG2_E = 1.4426950408889634

LANES = 128
SUBLANES = 8
VMEM_LIMIT_BYTES = 56 * 1024 * 1024

MLSTM_KERNEL_CHUNK = 256
MOE_ROW_TILE = 256
COMBINE_ROW_TILE = 128


def _params(semantics):
    return pltpu.CompilerParams(dimension_semantics=semantics, vmem_limit_bytes=VMEM_LIMIT_BYTES)


def _pick(n, target):
    t = min(n, target)
    while n % t:
        t //= 2
    return t


def _matmul_kernel(a_ref, b_ref, o_ref):
    o_ref[...] = jnp.dot(a_ref[...], b_ref[...].astype(BF16), preferred_element_type=F32).astype(o_ref.dtype)


def _matmul(a, b, layer, out_dtype, n_cols=None, tm=2048, tn=512):
    M, K = a.shape
    N = b.shape[2] if n_cols is None else n_cols
    tm, tn = _pick(M, tm), _pick(N, tn)
    return pl.pallas_call(
        _matmul_kernel,
        grid=(M // tm, N // tn),
        in_specs=[pl.BlockSpec((tm, K), lambda i, j: (i, 0), pipeline_mode=pl.Buffered(1)),
                  pl.BlockSpec((None, K, tn), lambda i, j: (layer, 0, j))],
        out_specs=pl.BlockSpec((tm, tn), lambda i, j: (i, j)),
        out_shape=jax.ShapeDtypeStruct((M, N), out_dtype),
        compiler_params=_params(("parallel", "parallel")),
        name="dense_matmul",
    )(a, b)


def _layer_norm_rows(z, g, b):
    mu = jnp.mean(z, axis=-1, keepdims=True)
    zc = z - mu
    var = jnp.mean(zc * zc, axis=-1, keepdims=True)
    return zc * lax.rsqrt(var + LN_EPS) * g + b


HIGH_HALF = 0xFFFF0000


def _pack_bf16_pair(lo, hi):
    lo_bits = lax.bitcast_convert_type(lo.astype(BF16).astype(F32), jnp.uint32)
    hi_bits = lax.bitcast_convert_type(hi.astype(BF16).astype(F32), jnp.uint32)
    return (hi_bits & jnp.uint32(HIGH_HALF)) | (lo_bits >> 16)


def _unpack_bf16_pair(w):
    lo = lax.bitcast_convert_type(w << 16, F32)
    hi = lax.bitcast_convert_type(w & jnp.uint32(HIGH_HALF), F32)
    return lo, hi


def _slab_pitch(chunks):
    pitch = -(-chunks // 4) * 4
    return pitch + 4 if pitch % 8 == 0 else pitch


def _store_slabs(ref, packed):
    n, half = packed.shape
    chunks = half // LANES
    pitch = ref.shape[0] // n
    for c in range(chunks):
        ref[pl.ds(c, n, stride=pitch), :] = packed[:, c * LANES:(c + 1) * LANES]
    for c in range(chunks, pitch):
        ref[pl.ds(c, n, stride=pitch), :] = jnp.zeros((n, LANES), packed.dtype)


def _load_slabs(ref, slot, first, n, chunks, pitch):
    return jnp.concatenate(
        [ref[slot, pl.ds(first * pitch + c, n, stride=pitch), :] for c in range(chunks)], axis=1)


def _route_rows(x, w, b, n_experts, topk):
    x_hi = x.astype(BF16)
    x_lo = (x - x_hi.astype(F32)).astype(BF16)
    w_hi = w.astype(BF16)
    w_lo = (w - w_hi.astype(F32)).astype(BF16)
    logits = (jnp.dot(x_hi, w_hi, preferred_element_type=F32) + jnp.dot(x_lo, w_hi, preferred_element_type=F32)
              + jnp.dot(x_hi, w_lo, preferred_element_type=F32)) + b
    lane = lax.broadcasted_iota(jnp.int32, logits.shape, 1)
    lane_f = lane.astype(F32)
    work = jnp.where(lane < n_experts, logits, -jnp.inf)
    idx_out = jnp.zeros(logits.shape, jnp.int32)
    val_out = jnp.full(logits.shape, -jnp.inf, F32)
    for kk in range(topk):
        mx = jnp.max(work, axis=1, keepdims=True)
        first = jnp.min(jnp.where(work == mx, lane_f, float(LANES)), axis=1, keepdims=True)
        first_i = first.astype(jnp.int32)
        idx_out = jnp.where(lane == kk, first_i, idx_out)
        val_out = jnp.where(lane == kk, mx, val_out)
        work = jnp.where(lane == first_i, -jnp.inf, work)
    e = jnp.exp(val_out - jnp.max(val_out, axis=1, keepdims=True))
    return idx_out, e / jnp.sum(e, axis=1, keepdims=True)


def _res_ln_route_kernel(x_ref, y_ref, g_ref, b_ref, rw_ref, rb_ref, of_ref, slab_ref, idx_ref, p_ref,
                         *, n_experts, topk):
    half = x_ref.shape[1] // 2
    z = DN_ALPHA * x_ref[...] + y_ref[...].astype(F32)
    out = _layer_norm_rows(z, g_ref[...], b_ref[...])
    of_ref[...] = out
    _store_slabs(slab_ref, _pack_bf16_pair(out[:, :half], out[:, half:]))
    idx_ref[...], p_ref[...] = _route_rows(out, rw_ref[...], rb_ref[...], n_experts, topk)


def _res_ln_route(x, y, g, b, router_w, router_b, topk, tm=256):
    M, D = x.shape
    n_experts = router_w.shape[1]
    tm = _pick(M, tm)
    pitch = _slab_pitch(D // 2 // LANES)
    w_pad = jnp.zeros((D, LANES), F32).at[:, :n_experts].set(router_w)
    b_pad = jnp.zeros((1, LANES), F32).at[0, :n_experts].set(router_b)
    row = pl.BlockSpec((tm, D), lambda i: (i, 0))
    vec = pl.BlockSpec((1, D), lambda i: (0, 0))
    lanes = pl.BlockSpec((tm, LANES), lambda i: (i, 0))
    kernel = functools.partial(_res_ln_route_kernel, n_experts=n_experts, topk=topk)
    return pl.pallas_call(
        kernel,
        grid=(M // tm,),
        in_specs=[row, row, vec, vec, pl.BlockSpec((D, LANES), lambda i: (0, 0)),
                  pl.BlockSpec((1, LANES), lambda i: (0, 0))],
        out_specs=[row, pl.BlockSpec((tm * pitch, LANES), lambda i: (i, 0)), lanes, lanes],
        out_shape=[jax.ShapeDtypeStruct((M, D), F32), jax.ShapeDtypeStruct((M * pitch, LANES), jnp.uint32),
                   jax.ShapeDtypeStruct((M, LANES), jnp.int32), jax.ShapeDtypeStruct((M, LANES), F32)],
        compiler_params=_params(("parallel",)),
        name="residual_layernorm_route",
    )(x, y, g.reshape(1, D), b.reshape(1, D), w_pad, b_pad)


def _moba_kernel(q_ref, k_ref, v_ref, o_ref, kmean_ref, vt_ref, s_ref, p_ref, *, nb, bs, topk, scale):
    c = scale * LOG2_E
    nt = (((1,), (1,)), ((), ()))
    neg_inf = -jnp.inf

    kmean_ref[...] = jnp.zeros_like(kmean_ref)
    for n in range(nb):
        blk = slice(n * bs, (n + 1) * bs)
        kmean_ref[n:n + 1, :] = jnp.mean(k_ref[blk, :].astype(F32), axis=0, keepdims=True)
        vt_ref[:, blk] = v_ref[blk, :].T

    kmean = kmean_ref[...]
    km_hi = kmean.astype(BF16)
    km_lo = (kmean - km_hi.astype(F32)).astype(BF16)
    q_all = q_ref[...]
    gate = (lax.dot_general(km_hi, q_all, nt, preferred_element_type=F32)
            + lax.dot_general(km_lo, q_all, nt, preferred_element_type=F32))

    def scores(j):
        s_ref[j % 2, 0:(j + 1) * bs, :] = lax.dot_general(
            k_ref[0:(j + 1) * bs, :], q_ref[j * bs:(j + 1) * bs, :], nt, preferred_element_type=F32)

    def softmax(j):
        slot = j % 2
        own = slice(j * bs, (j + 1) * bs)
        key_i = lax.broadcasted_iota(jnp.int32, (bs, bs), 0)
        qry_i = lax.broadcasted_iota(jnp.int32, (bs, bs), 1)
        causal = key_i <= qry_i
        m = jnp.max(jnp.where(causal, s_ref[slot, own, :], neg_inf), axis=0, keepdims=True)
        sel = [None] * j
        if j > topk:
            g = gate[:, own]
            blk_i = lax.broadcasted_iota(jnp.int32, g.shape, 0)
            g = jnp.where(blk_i < j, g, neg_inf)
        for n in range(j):
            bm = jnp.max(s_ref[slot, n * bs:(n + 1) * bs, :], axis=0, keepdims=True)
            if j > topk:
                gn = g[n:n + 1, :]
                beats = (g > gn) | ((g == gn) & (blk_i < n))
                sel[n] = jnp.sum(beats.astype(F32), axis=0, keepdims=True) < float(topk)
                bm = jnp.where(sel[n], bm, neg_inf)
            m = jnp.maximum(m, bm)
        p = jnp.exp2((jnp.where(causal, s_ref[slot, own, :], neg_inf) - m) * c)
        l = jnp.sum(p, axis=0, keepdims=True)
        p_ref[slot, own, :] = p.astype(BF16)
        for n in range(j):
            blk = slice(n * bs, (n + 1) * bs)
            m_use = m if sel[n] is None else jnp.where(sel[n], m, jnp.inf)
            p = jnp.exp2((s_ref[slot, blk, :] - m_use) * c)
            l = l + jnp.sum(p, axis=0, keepdims=True)
            p_ref[slot, blk, :] = p.astype(BF16)
        return l

    scores(0)
    for j in range(nb):
        if j + 1 < nb:
            scores(j + 1)
        l = softmax(j)
        nk = (j + 1) * bs
        acc = jnp.dot(vt_ref[:, 0:nk], p_ref[j % 2, 0:nk, :], preferred_element_type=F32)
        o_ref[j * bs:(j + 1) * bs, :] = (acc * (1.0 / l)).T.astype(o_ref.dtype)


def _moba_attention(qkv, batch, seq, heads, hd, bs, topk):
    assert seq % bs == 0 and hd % LANES == 0 and bs % LANES == 0
    nb = seq // bs
    nb_pad = -(-nb // SUBLANES) * SUBLANES
    kernel = functools.partial(_moba_kernel, nb=nb, bs=bs, topk=min(topk, nb), scale=hd ** -0.5)
    return pl.pallas_call(
        kernel,
        grid=(batch, heads),
        in_specs=[pl.BlockSpec((seq, hd), lambda b, h: (b, h)),
                  pl.BlockSpec((seq, hd), lambda b, h: (b, heads + h)),
                  pl.BlockSpec((seq, hd), lambda b, h: (b, 2 * heads + h))],
        out_specs=pl.BlockSpec((seq, hd), lambda b, h: (b, h)),
        out_shape=jax.ShapeDtypeStruct((batch * seq, heads * hd), BF16),
        scratch_shapes=[pltpu.VMEM((nb_pad, hd), F32), pltpu.VMEM((hd, seq), BF16),
                        pltpu.VMEM((2, seq, bs), F32), pltpu.VMEM((2, seq, bs), BF16)],
        compiler_params=_params(("parallel", "parallel")),
        name="moba_attention",
    )(qkv, qkv, qkv)


def _split3(x):
    h1 = x.astype(BF16)
    r1 = x - h1.astype(F32)
    h2 = r1.astype(BF16)
    r2 = r1 - h2.astype(F32)
    return h1, h2, r2.astype(BF16)


def _soft_cap(z):
    return MLSTM_GATE_CAP * jnp.tanh(z / MLSTM_GATE_CAP)


def _log_sigmoid(z):
    return jnp.minimum(z, 0.0) - jnp.log1p(jnp.exp(-jnp.abs(z)))


def _mlstm_kernel(q_ref, k_ref, v_ref, op_ref, gc_ref, gr_ref, bc_ref, br_ref, ng_ref, o_ref,
                  c_ref, n_ref, m_ref, *, heads):
    h = pl.program_id(1)
    c = pl.program_id(2)
    L, dk = q_ref.shape

    @pl.when(c == 0)
    def _():
        c_ref[...] = jnp.zeros_like(c_ref)
        n_ref[...] = jnp.zeros_like(n_ref)
        m_ref[...] = jnp.zeros_like(m_ref)

    gcol = _soft_cap(gc_ref[...] + bc_ref[...])
    lane = lax.broadcasted_iota(jnp.int32, gcol.shape, 1)
    li_col = jnp.sum(jnp.where(lane == h, gcol, 0.0), axis=1, keepdims=True)
    lf_col = _log_sigmoid(jnp.sum(jnp.where(lane == h + heads, gcol, 0.0), axis=1, keepdims=True))
    grow = _soft_cap(gr_ref[...] + br_ref[...])
    sub = lax.broadcasted_iota(jnp.int32, grow.shape, 0)
    li_row = jnp.sum(jnp.where(sub == h, grow, 0.0), axis=0, keepdims=True)
    lf_row = _log_sigmoid(jnp.sum(jnp.where(sub == h + heads, grow, 0.0), axis=0, keepdims=True))

    r_i = lax.broadcasted_iota(jnp.int32, (L, L), 0)
    c_i = lax.broadcasted_iota(jnp.int32, (L, L), 1)
    causal = c_i <= r_i
    tri = jnp.where(causal, 1.0, 0.0).astype(BF16)
    triu = jnp.where(r_i <= c_i, 1.0, 0.0).astype(BF16)
    lf_col_b = jnp.broadcast_to(lf_col, (L, LANES))
    b_col = sum(jnp.dot(tri, t, preferred_element_type=F32) for t in _split3(lf_col_b))[:, :1]
    lf_row_b = jnp.broadcast_to(lf_row, (SUBLANES, L))
    b_row = sum(jnp.dot(t, triu, preferred_element_type=F32) for t in _split3(lf_row_b))[:1, :]
    g_tot = b_col[L - 1:L, :]

    m_prev = m_ref[...]
    d = jnp.where(causal, b_col - b_row + li_row, -jnp.inf)
    m_inter = b_col + m_prev
    m_t = jnp.maximum(m_inter, jnp.max(d, axis=1, keepdims=True))
    w_inter = jnp.exp(m_inter - m_t)

    q = q_ref[...] * (dk ** -0.5)
    k = k_ref[...]
    v = v_ref[...]
    ones = jnp.ones((L, LANES), BF16)
    s = lax.dot_general(q, k, (((1,), (1,)), ((), ())), preferred_element_type=F32)
    a = (jnp.exp(d - m_t) * s).astype(BF16)
    num = (w_inter * jnp.dot(q, c_ref[...].astype(BF16), preferred_element_type=F32)
           + jnp.dot(a, v, preferred_element_type=F32))
    den = (w_inter * jnp.dot(q, n_ref[...].astype(BF16), preferred_element_type=F32)[:, :1]
           + jnp.dot(a, ones, preferred_element_type=F32)[:, :1])
    hh = num / jnp.maximum(jnp.abs(den), jnp.exp(-m_t))

    w_src = g_tot - b_col + li_col
    m_new = jnp.maximum(g_tot + m_prev, jnp.max(w_src, axis=0, keepdims=True))
    decay = jnp.exp(g_tot + m_prev - m_new)
    kw = (k.astype(F32) * jnp.exp(w_src - m_new)).astype(BF16)
    contract_rows = (((0,), (0,)), ((), ()))
    c_ref[...] = decay * c_ref[...] + lax.dot_general(kw, v, contract_rows, preferred_element_type=F32)
    n_ref[...] = decay * n_ref[...] + lax.dot_general(kw, ones, contract_rows, preferred_element_type=F32)
    m_ref[...] = m_new

    hn = hh * lax.rsqrt(jnp.mean(hh * hh, axis=1, keepdims=True) + RMS_EPS)
    o_ref[...] = (jax.nn.sigmoid(op_ref[...].astype(F32)) * (hn * ng_ref[...])).astype(o_ref.dtype)


def _mlstm_recurrence(proj, gates, b_gates, norm_g, batch, seq, heads, dk, dv, chunk):
    assert seq % chunk == 0 and dv % dk == 0
    nc = seq // chunk
    kernel = functools.partial(_mlstm_kernel, heads=heads)
    v0 = 2 * heads * dk // dv
    o0 = v0 + heads
    gates_t = gates.T
    return pl.pallas_call(
        kernel,
        grid=(batch, heads, nc),
        in_specs=[pl.BlockSpec((chunk, dk), lambda b, h, c: (b * nc + c, h)),
                  pl.BlockSpec((chunk, dk), lambda b, h, c: (b * nc + c, heads + h)),
                  pl.BlockSpec((chunk, dv), lambda b, h, c: (b * nc + c, v0 + h)),
                  pl.BlockSpec((chunk, dv), lambda b, h, c: (b * nc + c, o0 + h)),
                  pl.BlockSpec((chunk, 2 * heads), lambda b, h, c: (b * nc + c, 0)),
                  pl.BlockSpec((2 * heads, chunk), lambda b, h, c: (0, b * nc + c)),
                  pl.BlockSpec((1, 2 * heads), lambda b, h, c: (0, 0)),
                  pl.BlockSpec((2 * heads, 1), lambda b, h, c: (0, 0)),
                  pl.BlockSpec((1, dv), lambda b, h, c: (0, h))],
        out_specs=pl.BlockSpec((chunk, dv), lambda b, h, c: (b * nc + c, h)),
        out_shape=jax.ShapeDtypeStruct((batch * seq, heads * dv), BF16),
        scratch_shapes=[pltpu.VMEM((dk, dv), F32), pltpu.VMEM((dk, LANES), F32), pltpu.VMEM((1, 1), F32)],
        compiler_params=_params(("parallel", "parallel", "arbitrary")),
        name="mlstm_recurrence",
    )(proj, proj, proj, proj, gates, gates_t, b_gates.reshape(1, -1), b_gates.reshape(-1, 1),
      norm_g.reshape(1, -1))


def _start_slab_gather(src_hbm, row_ref, n_slabs, pitch, dst, sem):
    for r in range(n_slabs):
        row0 = pl.multiple_of(row_ref[0, 0, r], 4)
        pltpu.make_async_copy(src_hbm.at[pl.ds(row0, pitch), :], dst.at[pl.ds(r * pitch, pitch), :], sem).start()


def _wait_slab_gather(src_hbm, dst, sem):
    pltpu.make_async_copy(src_hbm.at[pl.ds(0, dst.shape[0]), :], dst, sem).wait()


WEIGHT_CAST_ROWS = 512


def _expert_kernel(te_ref, nxt_ref, nt_ref, tok_ref, tok_next_ref, xp_hbm, wgu_hbm, bgu_ref, wd_hbm, bd_ref, y_ref,
                   xbuf, sem, wgu_f32, wgu_bf, wd_f32, wd_bf, w_sem, *, ff, layer):
    t = pl.program_id(0)
    n_tiles = nt_ref[0]
    half = wd_bf.shape[1] // 2
    chunks = half // LANES
    pitch = _slab_pitch(chunks)
    tm = xbuf.shape[1] // pitch
    slot = lax.rem(t, 2)
    expert = te_ref[t]
    live = t < n_tiles
    fresh = jnp.logical_and(live, jnp.logical_or(t == 0, expert != te_ref[jnp.maximum(t - 1, 0)]))

    def weight_copies(e):
        return (pltpu.make_async_copy(wgu_hbm.at[layer, e], wgu_f32, w_sem.at[0]),
                pltpu.make_async_copy(wd_hbm.at[layer, e], wd_f32, w_sem.at[1]))

    @pl.when(t == 0)
    def _():
        _start_slab_gather(xp_hbm, tok_ref, tm, pitch, xbuf.at[0], sem.at[0])
        for copy in weight_copies(expert):
            copy.start()

    @pl.when(t + 1 < n_tiles)
    def _():
        _start_slab_gather(xp_hbm, tok_next_ref, tm, pitch, xbuf.at[1 - slot], sem.at[1 - slot])

    @pl.when(fresh)
    def _():
        gate_up_copy, down_copy = weight_copies(expert)
        gate_up_copy.wait()

        def cast_rows(i, carry):
            rows = pl.ds(pl.multiple_of(i * WEIGHT_CAST_ROWS, WEIGHT_CAST_ROWS), WEIGHT_CAST_ROWS)
            wgu_bf[rows, :] = wgu_f32[rows, :].astype(BF16)
            return carry
        lax.fori_loop(0, wgu_bf.shape[0] // WEIGHT_CAST_ROWS, cast_rows, 0)
        down_copy.wait()
        wd_bf[...] = wd_f32[...].astype(BF16)
        following = nxt_ref[t]

        @pl.when(following >= 0)
        def _():
            for copy in weight_copies(following):
                copy.start()

    @pl.when(live)
    def _():
        _wait_slab_gather(xp_hbm, xbuf.at[slot], sem.at[slot])
        x_lo, x_hi = _unpack_bf16_pair(_load_slabs(xbuf, slot, 0, tm, chunks, pitch))
        gu = (jnp.dot(x_lo.astype(BF16), wgu_bf[:half, :], preferred_element_type=F32)
              + jnp.dot(x_hi.astype(BF16), wgu_bf[half:, :], preferred_element_type=F32) + bgu_ref[0])
        gate = jnp.minimum(gu[:, :ff], SWIGLU_LIMIT)
        up = jnp.clip(gu[:, ff:], -SWIGLU_LIMIT, SWIGLU_LIMIT)
        act = ((up + 1.0) * gate * jax.nn.sigmoid(SWIGLU_ALPHA * gate)).astype(BF16)
        y = jnp.dot(act, wd_bf[...], preferred_element_type=F32) + bd_ref[0]
        _store_slabs(y_ref, _pack_bf16_pair(y[:, :half], y[:, half:]))

    @pl.when(jnp.logical_not(live))
    def _():
        y_ref[...] = jnp.zeros_like(y_ref)


def _expert_ffn(x_slabs, tile_expert, next_expert, n_tiles, tok_rows, w_gate_up, b_gate_up, w_down, b_down, layer,
                tm):
    n_layers, E, D, two_ff = w_gate_up.shape
    half = D // 2
    pitch = _slab_pitch(half // LANES)
    N = x_slabs.shape[0] // pitch
    ff = two_ff // 2
    n_tile_max = tile_expert.shape[0]
    assert N >= tm and D % WEIGHT_CAST_ROWS == 0
    kernel = functools.partial(_expert_kernel, ff=ff, layer=layer)
    tok_tiles = (tok_rows * pitch).reshape(n_tile_max, 1, tm)
    grid_spec = pltpu.PrefetchScalarGridSpec(
        num_scalar_prefetch=3,
        grid=(n_tile_max,),
        in_specs=[pl.BlockSpec((1, 1, tm), lambda t, te, nx, nt: (t, 0, 0), memory_space=pltpu.SMEM),
                  pl.BlockSpec((1, 1, tm), lambda t, te, nx, nt: (jnp.minimum(t + 1, n_tile_max - 1), 0, 0),
                               memory_space=pltpu.SMEM),
                  pl.BlockSpec(memory_space=pl.ANY),
                  pl.BlockSpec(memory_space=pl.ANY),
                  pl.BlockSpec((None, 1, 1, two_ff), lambda t, te, nx, nt: (layer, te[t], 0, 0)),
                  pl.BlockSpec(memory_space=pl.ANY),
                  pl.BlockSpec((None, 1, 1, D), lambda t, te, nx, nt: (layer, te[t], 0, 0))],
        out_specs=pl.BlockSpec((tm * pitch, LANES), lambda t, te, nx, nt: (t, 0)),
        scratch_shapes=[pltpu.VMEM((2, tm * pitch, LANES), jnp.uint32), pltpu.SemaphoreType.DMA((2,)),
                        pltpu.VMEM((D, two_ff), F32), pltpu.VMEM((D, two_ff), BF16),
                        pltpu.VMEM((ff, D), F32), pltpu.VMEM((ff, D), BF16),
                        pltpu.SemaphoreType.DMA((2,))],
    )
    return pl.pallas_call(
        kernel,
        grid_spec=grid_spec,
        out_shape=jax.ShapeDtypeStruct((n_tile_max * tm * pitch, LANES), jnp.uint32),
        compiler_params=_params(("arbitrary",)),
        name="moe_expert_ffn",
    )(tile_expert, next_expert, n_tiles, tok_tiles, tok_tiles, x_slabs,
      w_gate_up, b_gate_up.reshape(n_layers, E, 1, two_ff), w_down, b_down.reshape(n_layers, E, 1, D))


def _combine_kernel(pos_ref, pos_next_ref, p_ref, x_ref, y_hbm, g_ref, b_ref, of_ref, ob_ref, buf, sem, *, topk):
    i = pl.program_id(0)
    tc, d = x_ref.shape
    half = d // 2
    chunks = half // LANES
    pitch = _slab_pitch(chunks)
    slot = lax.rem(i, 2)

    @pl.when(i == 0)
    def _():
        _start_slab_gather(y_hbm, pos_ref, topk * tc, pitch, buf.at[0], sem.at[0])

    @pl.when(i + 1 < pl.num_programs(0))
    def _():
        _start_slab_gather(y_hbm, pos_next_ref, topk * tc, pitch, buf.at[1 - slot], sem.at[1 - slot])

    _wait_slab_gather(y_hbm, buf.at[slot], sem.at[slot])
    p = p_ref[...]
    z_lo = DN_ALPHA * x_ref[:, :half]
    z_hi = DN_ALPHA * x_ref[:, half:]
    for kk in range(topk):
        y_lo, y_hi = _unpack_bf16_pair(_load_slabs(buf, slot, kk * tc, tc, chunks, pitch))
        z_lo = z_lo + p[:, kk:kk + 1] * y_lo
        z_hi = z_hi + p[:, kk:kk + 1] * y_hi
    inv_d = 1.0 / (2 * half)
    mu = (jnp.sum(z_lo, axis=-1, keepdims=True) + jnp.sum(z_hi, axis=-1, keepdims=True)) * inv_d
    c_lo = z_lo - mu
    c_hi = z_hi - mu
    var = (jnp.sum(c_lo * c_lo, axis=-1, keepdims=True) + jnp.sum(c_hi * c_hi, axis=-1, keepdims=True)) * inv_d
    r = lax.rsqrt(var + LN_EPS)
    out_lo = c_lo * r * g_ref[:, :half] + b_ref[:, :half]
    out_hi = c_hi * r * g_ref[:, half:] + b_ref[:, half:]
    of_ref[:, :half] = out_lo
    of_ref[:, half:] = out_hi
    ob_ref[:, :half] = out_lo.astype(BF16)
    ob_ref[:, half:] = out_hi.astype(BF16)


def _combine_res_ln(x, y_slabs, pos, top_p, g, b, topk, tc):
    N, D = x.shape
    tc = _pick(N, tc)
    n_tiles = N // tc
    pitch = _slab_pitch(D // 2 // LANES)
    assert y_slabs.shape[0] >= topk * tc * pitch
    kernel = functools.partial(_combine_kernel, topk=topk)
    row = pl.BlockSpec((tc, D), lambda i: (i, 0))
    vec = pl.BlockSpec((1, D), lambda i: (0, 0))
    pos_tiles = (pos * pitch).reshape(n_tiles, tc, topk).transpose(0, 2, 1).reshape(n_tiles, 1, topk * tc)
    return pl.pallas_call(
        kernel,
        grid=(n_tiles,),
        in_specs=[pl.BlockSpec((1, 1, tc * topk), lambda i: (i, 0, 0), memory_space=pltpu.SMEM),
                  pl.BlockSpec((1, 1, tc * topk), lambda i: (jnp.minimum(i + 1, n_tiles - 1), 0, 0),
                               memory_space=pltpu.SMEM),
                  pl.BlockSpec((tc, LANES), lambda i: (i, 0)),
                  row,
                  pl.BlockSpec(memory_space=pl.ANY),
                  vec, vec],
        out_specs=[row, row],
        out_shape=[jax.ShapeDtypeStruct((N, D), F32), jax.ShapeDtypeStruct((N, D), BF16)],
        scratch_shapes=[pltpu.VMEM((2, topk * tc * pitch, LANES), jnp.uint32), pltpu.SemaphoreType.DMA((2,))],
        compiler_params=_params(("arbitrary",)),
        name="moe_combine_layernorm",
    )(pos_tiles, pos_tiles, top_p, x, y_slabs, g.reshape(1, D), b.reshape(1, D))


def _routing_tables(top_idx, n_experts, tm):
    N, K = top_idx.shape
    n_assign = N * K
    n_tile_max = n_assign // tm + n_experts
    e_flat = top_idx.reshape(-1)
    onehot = (e_flat[:, None] == jnp.arange(n_experts, dtype=jnp.int32)[None, :]).astype(jnp.int32)
    running = jnp.cumsum(onehot, axis=0)
    sizes = running[-1]
    rank = jnp.sum(running * onehot, axis=1) - 1
    padded = (sizes + tm - 1) // tm * tm
    pstart = jnp.cumsum(padded) - padded
    pos = pstart[e_flat] + rank
    n_tiles = (jnp.sum(padded) // tm).astype(jnp.int32).reshape(1)
    tile_start = jnp.arange(n_tile_max, dtype=jnp.int32) * tm
    pend = pstart + padded
    tile_expert = jnp.minimum(jnp.sum((tile_start[:, None] >= pend[None, :]).astype(jnp.int32), axis=1),
                              n_experts - 1).astype(jnp.int32)
    tok_rows = jnp.zeros((n_tile_max * tm,), jnp.int32).at[pos].set(
        jnp.arange(n_assign, dtype=jnp.int32) // K)
    expert_ids = jnp.arange(n_experts, dtype=jnp.int32)
    first_nonempty_from = lax.cummin(jnp.where(sizes > 0, expert_ids, n_experts), reverse=True)
    next_nonempty = jnp.concatenate([first_nonempty_from[1:], jnp.full((1,), n_experts, jnp.int32)])
    next_expert = next_nonempty[tile_expert]
    next_expert = jnp.where(next_expert < n_experts, next_expert, -1).astype(jnp.int32)
    return tile_expert, next_expert, n_tiles, tok_rows, pos.reshape(N, K).astype(jnp.int32)


def _post_mixer(x_f32, y, router_w, router_b, w_gate_up, b_gate_up, w_down, b_down, layer, ln_g, ln_b,
                topk=MOE_TOPK, tm=MOE_ROW_TILE, tc=COMBINE_ROW_TILE):
    n_experts = router_w.shape[1]
    h_f32, h_slabs, top_idx, top_p = _res_ln_route(x_f32, y, ln_g[0], ln_b[0], router_w, router_b, topk)
    tile_expert, next_expert, n_tiles, tok_rows, pos = _routing_tables(top_idx[:, :topk], n_experts, tm)
    y_slabs = _expert_ffn(h_slabs, tile_expert, next_expert, n_tiles, tok_rows, w_gate_up, b_gate_up, w_down, b_down,
                          layer, tm)
    return _combine_res_ln(h_f32, y_slabs, pos, top_p, ln_g[1], ln_b[1], topk, tc)


def _moba_mixer(x_bf16, w_in, w_out, layer, batch, seq, heads=MOBA_HEADS, hd=MOBA_HEAD_DIM, bs=MOBA_BLOCK,
                topk=MOBA_TOPK):
    qkv = _matmul(x_bf16, w_in, layer, BF16)
    o = _moba_attention(qkv, batch, seq, heads, hd, bs, topk)
    return _matmul(o, w_out, layer, BF16)


def _mlstm_mixer(x_bf16, w_in, b_gates, norm_g, w_out, layer, batch, seq, heads=MLSTM_HEADS,
                 dk=MLSTM_QK_DIM, dv=MLSTM_V_DIM, chunk=MLSTM_KERNEL_CHUNK):
    n_main = 2 * heads * (dk + dv)
    proj = _matmul(x_bf16, w_in, layer, BF16, n_cols=n_main)
    w_gates_pad = jnp.zeros((1, w_in.shape[1], LANES), F32).at[0, :, :2 * heads].set(w_in[layer, :, n_main:])
    gates = _matmul(x_bf16, w_gates_pad, 0, F32)[:, :2 * heads]
    hg = _mlstm_recurrence(proj, gates, b_gates, norm_g, batch, seq, heads, dk, dv, min(chunk, seq))
    return _matmul(hg, w_out, layer, BF16)


def kernel(x, moba_w_in, moba_w_out, mlstm_w_in, mlstm_b_gates, mlstm_norm_g, mlstm_w_out, ln_g, ln_b,
           router_w, router_b, moe_w_gate_up, moe_b_gate_up, moe_w_down, moe_b_down):
    batch, seq, d = x.shape
    x_f32 = x.reshape(batch * seq, d)
    x_bf16 = x_f32.astype(BF16)
    for i in range(DEPTH):
        slot = i // 2
        if i % 2 == 0:
            y = _moba_mixer(x_bf16, moba_w_in, moba_w_out, slot, batch, seq)
        else:
            y = _mlstm_mixer(x_bf16, mlstm_w_in, mlstm_b_gates[slot], mlstm_norm_g[slot], mlstm_w_out, slot,
                             batch, seq)
        x_f32, x_bf16 = _post_mixer(x_f32, y, router_w[i], router_b[i], moe_w_gate_up, moe_b_gate_up,
                                    moe_w_down, moe_b_down, i, ln_g[i], ln_b[i])
    return x_f32.reshape(batch, seq, d)
```

```python
import functools

import jax
import jax.numpy as jnp
from jax import lax
from jax.experimental import pallas as pl
from jax.experimental.pallas import tpu as pltpu

F32 = jnp.float32
BF16 = jnp.bfloat16

D_MODEL = 4096
DEPTH = 4
MOBA_HEADS = 32
MOBA_HEAD_DIM = D_MODEL // MOBA_HEADS
MOBA_BLOCK = 256
MOBA_TOPK = 3
MLSTM_HEADS = 8
MLSTM_QK_DIM = D_MODEL // 2 // MLSTM_HEADS
MLSTM_V_DIM = D_MODEL // MLSTM_HEADS
MLSTM_GATE_CAP = 15.0
N_EXPERTS = 32
MOE_TOPK = 4
EXPERT_FF = 384
SWIGLU_LIMIT = 7.0
SWIGLU_ALPHA = 1.702
DN_ALPHA = (2 * DEPTH) ** 0.25
LN_EPS = 1e-5
RMS_EPS = 1e-6
LOG2_E = 1.4426950408889634

LANES = 128
SUBLANES = 8
VMEM_LIMIT_BYTES = 56 * 1024 * 1024

MLSTM_KERNEL_CHUNK = 256
MOE_ROW_TILE = 256
COMBINE_ROW_TILE = 128


def _params(semantics):
    return pltpu.CompilerParams(dimension_semantics=semantics, vmem_limit_bytes=VMEM_LIMIT_BYTES)


def _pick(n, target):
    t = min(n, target)
    while n % t:
        t //= 2
    return t


def _matmul_kernel(a_ref, b_ref, o_ref):
    o_ref[...] = jnp.dot(a_ref[...], b_ref[...].astype(BF16), preferred_element_type=F32).astype(o_ref.dtype)


def _matmul(a, b, layer, out_dtype, n_cols=None, tm=2048, tn=512):
    M, K = a.shape
    N = b.shape[2] if n_cols is None else n_cols
    tm, tn = _pick(M, tm), _pick(N, tn)
    return pl.pallas_call(
        _matmul_kernel,
        grid=(M // tm, N // tn),
        in_specs=[pl.BlockSpec((tm, K), lambda i, j: (i, 0), pipeline_mode=pl.Buffered(1)),
                  pl.BlockSpec((None, K, tn), lambda i, j: (layer, 0, j))],
        out_specs=pl.BlockSpec((tm, tn), lambda i, j: (i, j)),
        out_shape=jax.ShapeDtypeStruct((M, N), out_dtype),
        compiler_params=_params(("parallel", "parallel")),
        name="dense_matmul",
    )(a, b)


def _layer_norm_rows(z, g, b):
    mu = jnp.mean(z, axis=-1, keepdims=True)
    zc = z - mu
    var = jnp.mean(zc * zc, axis=-1, keepdims=True)
    return zc * lax.rsqrt(var + LN_EPS) * g + b


def _slab_pitch(chunks):
    return chunks + 1 if chunks % SUBLANES == 0 else chunks


def _store_slabs(ref, rows):
    n, width = rows.shape
    chunks = width // LANES
    pitch = ref.shape[0] // n
    for c in range(chunks):
        ref[pl.ds(c, n, stride=pitch), :] = rows[:, c * LANES:(c + 1) * LANES]
    for c in range(chunks, pitch):
        ref[pl.ds(c, n, stride=pitch), :] = jnp.zeros((n, LANES), rows.dtype)


def _load_slabs(ref, slot, first, n, chunks, pitch):
    return jnp.concatenate(
        [ref[slot, pl.ds(first * pitch + c, n, stride=pitch), :] for c in range(chunks)], axis=1)


def _route_rows(x, w, b, n_experts, topk):
    x_hi = x.astype(BF16)
    x_lo = (x - x_hi.astype(F32)).astype(BF16)
    w_hi = w.astype(BF16)
    w_lo = (w - w_hi.astype(F32)).astype(BF16)
    logits = (jnp.dot(x_hi, w_hi, preferred_element_type=F32) + jnp.dot(x_lo, w_hi, preferred_element_type=F32)
              + jnp.dot(x_hi, w_lo, preferred_element_type=F32)) + b
    lane = lax.broadcasted_iota(jnp.int32, logits.shape, 1)
    lane_f = lane.astype(F32)
    work = jnp.where(lane < n_experts, logits, -jnp.inf)
    idx_out = jnp.zeros(logits.shape, jnp.int32)
    val_out = jnp.full(logits.shape, -jnp.inf, F32)
    for kk in range(topk):
        mx = jnp.max(work, axis=1, keepdims=True)
        first = jnp.min(jnp.where(work == mx, lane_f, float(LANES)), axis=1, keepdims=True)
        first_i = first.astype(jnp.int32)
        idx_out = jnp.where(lane == kk, first_i, idx_out)
        val_out = jnp.where(lane == kk, mx, val_out)
        work = jnp.where(lane == first_i, -jnp.inf, work)
    e = jnp.exp(val_out - jnp.max(val_out, axis=1, keepdims=True))
    return idx_out, e / jnp.sum(e, axis=1, keepdims=True)


def _res_ln_route_kernel(x_ref, y_ref, g_ref, b_ref, rw_ref, rb_ref, of_ref, slab_ref, idx_ref, p_ref,
                         *, n_experts, topk):
    z = DN_ALPHA * x_ref[...] + y_ref[...].astype(F32)
    out = _layer_norm_rows(z, g_ref[...], b_ref[...])
    of_ref[...] = out
    _store_slabs(slab_ref, out)
    idx_ref[...], p_ref[...] = _route_rows(out, rw_ref[...], rb_ref[...], n_experts, topk)


def _res_ln_route(x, y, g, b, router_w, router_b, topk, tm=256):
    M, D = x.shape
    n_experts = router_w.shape[1]
    tm = _pick(M, tm)
    pitch = _slab_pitch(D // LANES)
    w_pad = jnp.zeros((D, LANES), F32).at[:, :n_experts].set(router_w)
    b_pad = jnp.zeros((1, LANES), F32).at[0, :n_experts].set(router_b)
    row = pl.BlockSpec((tm, D), lambda i: (i, 0))
    vec = pl.BlockSpec((1, D), lambda i: (0, 0))
    lanes = pl.BlockSpec((tm, LANES), lambda i: (i, 0))
    kernel = functools.partial(_res_ln_route_kernel, n_experts=n_experts, topk=topk)
    return pl.pallas_call(
        kernel,
        grid=(M // tm,),
        in_specs=[row, row, vec, vec, pl.BlockSpec((D, LANES), lambda i: (0, 0)),
                  pl.BlockSpec((1, LANES), lambda i: (0, 0))],
        out_specs=[row, pl.BlockSpec((tm * pitch, LANES), lambda i: (i, 0)), lanes, lanes],
        out_shape=[jax.ShapeDtypeStruct((M, D), F32), jax.ShapeDtypeStruct((M * pitch, LANES), F32),
                   jax.ShapeDtypeStruct((M, LANES), jnp.int32), jax.ShapeDtypeStruct((M, LANES), F32)],
        compiler_params=_params(("parallel",)),
        name="residual_layernorm_route",
    )(x, y, g.reshape(1, D), b.reshape(1, D), w_pad, b_pad)


def _moba_kernel(q_ref, k_ref, v_ref, o_ref, kmean_ref, vt_ref, s_ref, p_ref, *, nb, bs, topk, scale):
    c = scale * LOG2_E
    nt = (((1,), (1,)), ((), ()))
    neg_inf = -jnp.inf

    kmean_ref[...] = jnp.zeros_like(kmean_ref)
    for n in range(nb):
        blk = slice(n * bs, (n + 1) * bs)
        kmean_ref[n:n + 1, :] = jnp.mean(k_ref[blk, :].astype(F32), axis=0, keepdims=True)
        vt_ref[:, blk] = v_ref[blk, :].T

    kmean = kmean_ref[...]
    km_hi = kmean.astype(BF16)
    km_lo = (kmean - km_hi.astype(F32)).astype(BF16)
    q_all = q_ref[...]
    gate = (lax.dot_general(km_hi, q_all, nt, preferred_element_type=F32)
            + lax.dot_general(km_lo, q_all, nt, preferred_element_type=F32))

    def scores(j):
        s_ref[j % 2, 0:(j + 1) * bs, :] = lax.dot_general(
            k_ref[0:(j + 1) * bs, :], q_ref[j * bs:(j + 1) * bs, :], nt, preferred_element_type=F32)

    def softmax(j):
        slot = j % 2
        own = slice(j * bs, (j + 1) * bs)
        key_i = lax.broadcasted_iota(jnp.int32, (bs, bs), 0)
        qry_i = lax.broadcasted_iota(jnp.int32, (bs, bs), 1)
        causal = key_i <= qry_i
        m = jnp.max(jnp.where(causal, s_ref[slot, own, :], neg_inf), axis=0, keepdims=True)
        sel = [None] * j
        if j > topk:
            g = gate[:, own]
            blk_i = lax.broadcasted_iota(jnp.int32, g.shape, 0)
            g = jnp.where(blk_i < j, g, neg_inf)
        for n in range(j):
            bm = jnp.max(s_ref[slot, n * bs:(n + 1) * bs, :], axis=0, keepdims=True)
            if j > topk:
                gn = g[n:n + 1, :]
                beats = (g > gn) | ((g == gn) & (blk_i < n))
                sel[n] = jnp.sum(beats.astype(F32), axis=0, keepdims=True) < float(topk)
                bm = jnp.where(sel[n], bm, neg_inf)
            m = jnp.maximum(m, bm)
        p = jnp.exp2((jnp.where(causal, s_ref[slot, own, :], neg_inf) - m) * c)
        l = jnp.sum(p, axis=0, keepdims=True)
        p_ref[slot, own, :] = p.astype(BF16)
        for n in range(j):
            blk = slice(n * bs, (n + 1) * bs)
            m_use = m if sel[n] is None else jnp.where(sel[n], m, jnp.inf)
            p = jnp.exp2((s_ref[slot, blk, :] - m_use) * c)
            l = l + jnp.sum(p, axis=0, keepdims=True)
            p_ref[slot, blk, :] = p.astype(BF16)
        return l

    scores(0)
    for j in range(nb):
        if j + 1 < nb:
            scores(j + 1)
        l = softmax(j)
        nk = (j + 1) * bs
        acc = jnp.dot(vt_ref[:, 0:nk], p_ref[j % 2, 0:nk, :], preferred_element_type=F32)
        o_ref[j * bs:(j + 1) * bs, :] = (acc * (1.0 / l)).T.astype(o_ref.dtype)


def _moba_attention(qkv, batch, seq, heads, hd, bs, topk):
    assert seq % bs == 0 and hd % LANES == 0 and bs % LANES == 0
    nb = seq // bs
    nb_pad = -(-nb // SUBLANES) * SUBLANES
    kernel = functools.partial(_moba_kernel, nb=nb, bs=bs, topk=min(topk, nb), scale=hd ** -0.5)
    return pl.pallas_call(
        kernel,
        grid=(batch, heads),
        in_specs=[pl.BlockSpec((seq, hd), lambda b, h: (b, h)),
                  pl.BlockSpec((seq, hd), lambda b, h: (b, heads + h)),
                  pl.BlockSpec((seq, hd), lambda b, h: (b, 2 * heads + h))],
        out_specs=pl.BlockSpec((seq, hd), lambda b, h: (b, h)),
        out_shape=jax.ShapeDtypeStruct((batch * seq, heads * hd), BF16),
        scratch_shapes=[pltpu.VMEM((nb_pad, hd), F32), pltpu.VMEM((hd, seq), BF16),
                        pltpu.VMEM((2, seq, bs), F32), pltpu.VMEM((2, seq, bs), BF16)],
        compiler_params=_params(("parallel", "parallel")),
        name="moba_attention",
    )(qkv, qkv, qkv)


def _split3(x):
    h1 = x.astype(BF16)
    r1 = x - h1.astype(F32)
    h2 = r1.astype(BF16)
    r2 = r1 - h2.astype(F32)
    return h1, h2, r2.astype(BF16)


def _soft_cap(z):
    return MLSTM_GATE_CAP * jnp.tanh(z / MLSTM_GATE_CAP)


def _log_sigmoid(z):
    return jnp.minimum(z, 0.0) - jnp.log1p(jnp.exp(-jnp.abs(z)))


def _mlstm_kernel(q_ref, k_ref, v_ref, op_ref, gc_ref, gr_ref, bc_ref, br_ref, ng_ref, o_ref,
                  c_ref, n_ref, m_ref, *, heads):
    h = pl.program_id(1)
    c = pl.program_id(2)
    L, dk = q_ref.shape

    @pl.when(c == 0)
    def _():
        c_ref[...] = jnp.zeros_like(c_ref)
        n_ref[...] = jnp.zeros_like(n_ref)
        m_ref[...] = jnp.zeros_like(m_ref)

    gcol = _soft_cap(gc_ref[...] + bc_ref[...])
    lane = lax.broadcasted_iota(jnp.int32, gcol.shape, 1)
    li_col = jnp.sum(jnp.where(lane == h, gcol, 0.0), axis=1, keepdims=True)
    lf_col = _log_sigmoid(jnp.sum(jnp.where(lane == h + heads, gcol, 0.0), axis=1, keepdims=True))
    grow = _soft_cap(gr_ref[...] + br_ref[...])
    sub = lax.broadcasted_iota(jnp.int32, grow.shape, 0)
    li_row = jnp.sum(jnp.where(sub == h, grow, 0.0), axis=0, keepdims=True)
    lf_row = _log_sigmoid(jnp.sum(jnp.where(sub == h + heads, grow, 0.0), axis=0, keepdims=True))

    r_i = lax.broadcasted_iota(jnp.int32, (L, L), 0)
    c_i = lax.broadcasted_iota(jnp.int32, (L, L), 1)
    causal = c_i <= r_i
    tri = jnp.where(causal, 1.0, 0.0).astype(BF16)
    triu = jnp.where(r_i <= c_i, 1.0, 0.0).astype(BF16)
    lf_col_b = jnp.broadcast_to(lf_col, (L, LANES))
    b_col = sum(jnp.dot(tri, t, preferred_element_type=F32) for t in _split3(lf_col_b))[:, :1]
    lf_row_b = jnp.broadcast_to(lf_row, (SUBLANES, L))
    b_row = sum(jnp.dot(t, triu, preferred_element_type=F32) for t in _split3(lf_row_b))[:1, :]
    g_tot = b_col[L - 1:L, :]

    m_prev = m_ref[...]
    d = jnp.where(causal, b_col - b_row + li_row, -jnp.inf)
    m_inter = b_col + m_prev
    m_t = jnp.maximum(m_inter, jnp.max(d, axis=1, keepdims=True))
    w_inter = jnp.exp(m_inter - m_t)

    q = q_ref[...] * (dk ** -0.5)
    k = k_ref[...]
    v = v_ref[...]
    ones = jnp.ones((L, LANES), BF16)
    s = lax.dot_general(q, k, (((1,), (1,)), ((), ())), preferred_element_type=F32)
    a = (jnp.exp(d - m_t) * s).astype(BF16)
    num = (w_inter * jnp.dot(q, c_ref[...].astype(BF16), preferred_element_type=F32)
           + jnp.dot(a, v, preferred_element_type=F32))
    den = (w_inter * jnp.dot(q, n_ref[...].astype(BF16), preferred_element_type=F32)[:, :1]
           + jnp.dot(a, ones, preferred_element_type=F32)[:, :1])
    hh = num / jnp.maximum(jnp.abs(den), jnp.exp(-m_t))

    w_src = g_tot - b_col + li_col
    m_new = jnp.maximum(g_tot + m_prev, jnp.max(w_src, axis=0, keepdims=True))
    decay = jnp.exp(g_tot + m_prev - m_new)
    kw = (k.astype(F32) * jnp.exp(w_src - m_new)).astype(BF16)
    contract_rows = (((0,), (0,)), ((), ()))
    c_ref[...] = decay * c_ref[...] + lax.dot_general(kw, v, contract_rows, preferred_element_type=F32)
    n_ref[...] = decay * n_ref[...] + lax.dot_general(kw, ones, contract_rows, preferred_element_type=F32)
    m_ref[...] = m_new

    hn = hh * lax.rsqrt(jnp.mean(hh * hh, axis=1, keepdims=True) + RMS_EPS)
    o_ref[...] = (jax.nn.sigmoid(op_ref[...].astype(F32)) * (hn * ng_ref[...])).astype(o_ref.dtype)


def _mlstm_recurrence(proj, gates, b_gates, norm_g, batch, seq, heads, dk, dv, chunk):
    assert seq % chunk == 0 and dv % dk == 0
    nc = seq // chunk
    kernel = functools.partial(_mlstm_kernel, heads=heads)
    v0 = 2 * heads * dk // dv
    o0 = v0 + heads
    gates_t = gates.T
    return pl.pallas_call(
        kernel,
        grid=(batch, heads, nc),
        in_specs=[pl.BlockSpec((chunk, dk), lambda b, h, c: (b * nc + c, h)),
                  pl.BlockSpec((chunk, dk), lambda b, h, c: (b * nc + c, heads + h)),
                  pl.BlockSpec((chunk, dv), lambda b, h, c: (b * nc + c, v0 + h)),
                  pl.BlockSpec((chunk, dv), lambda b, h, c: (b * nc + c, o0 + h)),
                  pl.BlockSpec((chunk, 2 * heads), lambda b, h, c: (b * nc + c, 0)),
                  pl.BlockSpec((2 * heads, chunk), lambda b, h, c: (0, b * nc + c)),
                  pl.BlockSpec((1, 2 * heads), lambda b, h, c: (0, 0)),
                  pl.BlockSpec((2 * heads, 1), lambda b, h, c: (0, 0)),
                  pl.BlockSpec((1, dv), lambda b, h, c: (0, h))],
        out_specs=pl.BlockSpec((chunk, dv), lambda b, h, c: (b * nc + c, h)),
        out_shape=jax.ShapeDtypeStruct((batch * seq, heads * dv), BF16),
        scratch_shapes=[pltpu.VMEM((dk, dv), F32), pltpu.VMEM((dk, LANES), F32), pltpu.VMEM((1, 1), F32)],
        compiler_params=_params(("parallel", "parallel", "arbitrary")),
        name="mlstm_recurrence",
    )(proj, proj, proj, proj, gates, gates_t, b_gates.reshape(1, -1), b_gates.reshape(-1, 1),
      norm_g.reshape(1, -1))


GATHER_PRIORITY = 0
WEIGHT_COPY_PRIORITY = 1


def _start_slab_gather(src_hbm, row_ref, n_slabs, pitch, dst, sem, alternate_queues):
    for r in range(n_slabs):
        copy = pltpu.make_async_copy(src_hbm.at[pl.ds(row_ref[0, 0, r], pitch), :],
                                     dst.at[pl.ds(r * pitch, pitch), :], sem)
        copy.start(priority=r % 2 if alternate_queues else GATHER_PRIORITY)


def _wait_slab_gather(src_hbm, dst, sem):
    pltpu.make_async_copy(src_hbm.at[pl.ds(0, dst.shape[0]), :], dst, sem).wait()


WEIGHT_CAST_ROWS = 512


def _expert_kernel(te_ref, nxt_ref, nt_ref, tok_ref, tok_next_ref, xp_hbm, wgu_hbm, bgu_ref, wd_hbm, bd_ref, y_ref,
                   xbuf, sem, wgu_f32, wgu_bf, wd_f32, wd_bf, w_sem, *, ff, layer):
    t = pl.program_id(0)
    n_tiles = nt_ref[0]
    chunks = wd_bf.shape[1] // LANES
    pitch = _slab_pitch(chunks)
    tm = xbuf.shape[1] // pitch
    slot = lax.rem(t, 2)
    expert = te_ref[t]
    live = t < n_tiles
    fresh = jnp.logical_and(live, jnp.logical_or(t == 0, expert != te_ref[jnp.maximum(t - 1, 0)]))

    def weight_copies(e):
        return (pltpu.make_async_copy(wgu_hbm.at[layer, e], wgu_f32, w_sem.at[0]),
                pltpu.make_async_copy(wd_hbm.at[layer, e], wd_f32, w_sem.at[1]))

    @pl.when(t == 0)
    def _():
        _start_slab_gather(xp_hbm, tok_ref, tm, pitch, xbuf.at[0], sem.at[0], alternate_queues=False)
        for copy in weight_copies(expert):
            copy.start(priority=WEIGHT_COPY_PRIORITY)

    @pl.when(t + 1 < n_tiles)
    def _():
        _start_slab_gather(xp_hbm, tok_next_ref, tm, pitch, xbuf.at[1 - slot], sem.at[1 - slot],
                           alternate_queues=False)

    @pl.when(fresh)
    def _():
        gate_up_copy, down_copy = weight_copies(expert)
        gate_up_copy.wait()

        def cast_rows(i, carry):
            rows = pl.ds(pl.multiple_of(i * WEIGHT_CAST_ROWS, WEIGHT_CAST_ROWS), WEIGHT_CAST_ROWS)
            wgu_bf[rows, :] = wgu_f32[rows, :].astype(BF16)
            return carry
        lax.fori_loop(0, wgu_bf.shape[0] // WEIGHT_CAST_ROWS, cast_rows, 0)
        down_copy.wait()
        wd_bf[...] = wd_f32[...].astype(BF16)
        following = nxt_ref[t]

        @pl.when(following >= 0)
        def _():
            for copy in weight_copies(following):
                copy.start(priority=WEIGHT_COPY_PRIORITY)

    @pl.when(live)
    def _():
        _wait_slab_gather(xp_hbm, xbuf.at[slot], sem.at[slot])
        x = _load_slabs(xbuf, slot, 0, tm, chunks, pitch).astype(BF16)
        gu = jnp.dot(x, wgu_bf[...], preferred_element_type=F32) + bgu_ref[0]
        gate = jnp.minimum(gu[:, :ff], SWIGLU_LIMIT)
        up = jnp.clip(gu[:, ff:], -SWIGLU_LIMIT, SWIGLU_LIMIT)
        act = ((up + 1.0) * gate * jax.nn.sigmoid(SWIGLU_ALPHA * gate)).astype(BF16)
        y = jnp.dot(act, wd_bf[...], preferred_element_type=F32) + bd_ref[0]
        _store_slabs(y_ref, y)

    @pl.when(jnp.logical_not(live))
    def _():
        y_ref[...] = jnp.zeros_like(y_ref)


def _expert_ffn(x_slabs, tile_expert, next_expert, n_tiles, tok_rows, w_gate_up, b_gate_up, w_down, b_down, layer,
                tm):
    n_layers, E, D, two_ff = w_gate_up.shape
    pitch = _slab_pitch(D // LANES)
    N = x_slabs.shape[0] // pitch
    ff = two_ff // 2
    n_tile_max = tile_expert.shape[0]
    assert N >= tm and D % WEIGHT_CAST_ROWS == 0
    kernel = functools.partial(_expert_kernel, ff=ff, layer=layer)
    tok_tiles = (tok_rows * pitch).reshape(n_tile_max, 1, tm)
    grid_spec = pltpu.PrefetchScalarGridSpec(
        num_scalar_prefetch=3,
        grid=(n_tile_max,),
        in_specs=[pl.BlockSpec((1, 1, tm), lambda t, te, nx, nt: (t, 0, 0), memory_space=pltpu.SMEM),
                  pl.BlockSpec((1, 1, tm), lambda t, te, nx, nt: (jnp.minimum(t + 1, n_tile_max - 1), 0, 0),
                               memory_space=pltpu.SMEM),
                  pl.BlockSpec(memory_space=pl.ANY),
                  pl.BlockSpec(memory_space=pl.ANY),
                  pl.BlockSpec((None, 1, 1, two_ff), lambda t, te, nx, nt: (layer, te[t], 0, 0)),
                  pl.BlockSpec(memory_space=pl.ANY),
                  pl.BlockSpec((None, 1, 1, D), lambda t, te, nx, nt: (layer, te[t], 0, 0))],
        out_specs=pl.BlockSpec((tm * pitch, LANES), lambda t, te, nx, nt: (t, 0)),
        scratch_shapes=[pltpu.VMEM((2, tm * pitch, LANES), F32), pltpu.SemaphoreType.DMA((2,)),
                        pltpu.VMEM((D, two_ff), F32), pltpu.VMEM((D, two_ff), BF16),
                        pltpu.VMEM((ff, D), F32), pltpu.VMEM((ff, D), BF16),
                        pltpu.SemaphoreType.DMA((2,))],
    )
    return pl.pallas_call(
        kernel,
        grid_spec=grid_spec,
        out_shape=jax.ShapeDtypeStruct((n_tile_max * tm * pitch, LANES), F32),
        compiler_params=_params(("arbitrary",)),
        name="moe_expert_ffn",
    )(tile_expert, next_expert, n_tiles, tok_tiles, tok_tiles, x_slabs,
      w_gate_up, b_gate_up.reshape(n_layers, E, 1, two_ff), w_down, b_down.reshape(n_layers, E, 1, D))


def _combine_kernel(pos_ref, pos_next_ref, p_ref, x_ref, y_hbm, g_ref, b_ref, of_ref, ob_ref, buf, sem, *, topk):
    i = pl.program_id(0)
    tc, d = x_ref.shape
    chunks = d // LANES
    pitch = _slab_pitch(chunks)
    slot = lax.rem(i, 2)

    @pl.when(i == 0)
    def _():
        _start_slab_gather(y_hbm, pos_ref, topk * tc, pitch, buf.at[0], sem.at[0], alternate_queues=True)

    @pl.when(i + 1 < pl.num_programs(0))
    def _():
        _start_slab_gather(y_hbm, pos_next_ref, topk * tc, pitch, buf.at[1 - slot], sem.at[1 - slot],
                           alternate_queues=True)

    _wait_slab_gather(y_hbm, buf.at[slot], sem.at[slot])
    p = p_ref[...]
    z = DN_ALPHA * x_ref[...]
    for kk in range(topk):
        z = z + p[:, kk:kk + 1] * _load_slabs(buf, slot, kk * tc, tc, chunks, pitch)
    out = _layer_norm_rows(z, g_ref[...], b_ref[...])
    of_ref[...] = out
    ob_ref[...] = out.astype(BF16)


def _combine_res_ln(x, y_slabs, pos, top_p, g, b, topk, tc):
    N, D = x.shape
    tc = _pick(N, tc)
    n_tiles = N // tc
    pitch = _slab_pitch(D // LANES)
    assert y_slabs.shape[0] >= topk * tc * pitch
    kernel = functools.partial(_combine_kernel, topk=topk)
    row = pl.BlockSpec((tc, D), lambda i: (i, 0))
    vec = pl.BlockSpec((1, D), lambda i: (0, 0))
    pos_tiles = (pos * pitch).reshape(n_tiles, tc, topk).transpose(0, 2, 1).reshape(n_tiles, 1, topk * tc)
    return pl.pallas_call(
        kernel,
        grid=(n_tiles,),
        in_specs=[pl.BlockSpec((1, 1, tc * topk), lambda i: (i, 0, 0), memory_space=pltpu.SMEM),
                  pl.BlockSpec((1, 1, tc * topk), lambda i: (jnp.minimum(i + 1, n_tiles - 1), 0, 0),
                               memory_space=pltpu.SMEM),
                  pl.BlockSpec((tc, LANES), lambda i: (i, 0)),
                  row,
                  pl.BlockSpec(memory_space=pl.ANY),
                  vec, vec],
        out_specs=[row, row],
        out_shape=[jax.ShapeDtypeStruct((N, D), F32), jax.ShapeDtypeStruct((N, D), BF16)],
        scratch_shapes=[pltpu.VMEM((2, topk * tc * pitch, LANES), F32), pltpu.SemaphoreType.DMA((2,))],
        compiler_params=_params(("arbitrary",)),
        name="moe_combine_layernorm",
    )(pos_tiles, pos_tiles, top_p, x, y_slabs, g.reshape(1, D), b.reshape(1, D))


def _routing_tables(top_idx, n_experts, tm):
    N, K = top_idx.shape
    n_assign = N * K
    n_tile_max = n_assign // tm + n_experts
    e_flat = top_idx.reshape(-1)
    onehot = (e_flat[:, None] == jnp.arange(n_experts, dtype=jnp.int32)[None, :]).astype(jnp.int32)
    running = jnp.cumsum(onehot, axis=0)
    sizes = running[-1]
    rank = jnp.sum(running * onehot, axis=1) - 1
    padded = (sizes + tm - 1) // tm * tm
    pstart = jnp.cumsum(padded) - padded
    pos = pstart[e_flat] + rank
    n_tiles = (jnp.sum(padded) // tm).astype(jnp.int32).reshape(1)
    tile_start = jnp.arange(n_tile_max, dtype=jnp.int32) * tm
    pend = pstart + padded
    tile_expert = jnp.minimum(jnp.sum((tile_start[:, None] >= pend[None, :]).astype(jnp.int32), axis=1),
                              n_experts - 1).astype(jnp.int32)
    tok_rows = jnp.zeros((n_tile_max * tm,), jnp.int32).at[pos].set(
        jnp.arange(n_assign, dtype=jnp.int32) // K)
    expert_ids = jnp.arange(n_experts, dtype=jnp.int32)
    first_nonempty_from = lax.cummin(jnp.where(sizes > 0, expert_ids, n_experts), reverse=True)
    next_nonempty = jnp.concatenate([first_nonempty_from[1:], jnp.full((1,), n_experts, jnp.int32)])
    next_expert = next_nonempty[tile_expert]
    next_expert = jnp.where(next_expert < n_experts, next_expert, -1).astype(jnp.int32)
    return tile_expert, next_expert, n_tiles, tok_rows, pos.reshape(N, K).astype(jnp.int32)


def _post_mixer(x_f32, y, router_w, router_b, w_gate_up, b_gate_up, w_down, b_down, layer, ln_g, ln_b,
                topk=MOE_TOPK, tm=MOE_ROW_TILE, tc=COMBINE_ROW_TILE):
    n_experts = router_w.shape[1]
    h_f32, h_slabs, top_idx, top_p = _res_ln_route(x_f32, y, ln_g[0], ln_b[0], router_w, router_b, topk)
    tile_expert, next_expert, n_tiles, tok_rows, pos = _routing_tables(top_idx[:, :topk], n_experts, tm)
    y_slabs = _expert_ffn(h_slabs, tile_expert, next_expert, n_tiles, tok_rows, w_gate_up, b_gate_up, w_down, b_down,
                          layer, tm)
    return _combine_res_ln(h_f32, y_slabs, pos, top_p, ln_g[1], ln_b[1], topk, tc)


def _moba_mixer(x_bf16, w_in, w_out, layer, batch, seq, heads=MOBA_HEADS, hd=MOBA_HEAD_DIM, bs=MOBA_BLOCK,
                topk=MOBA_TOPK):
    qkv = _matmul(x_bf16, w_in, layer, BF16)
    o = _moba_attention(qkv, batch, seq, heads, hd, bs, topk)
    return _matmul(o, w_out, layer, BF16)


def _mlstm_mixer(x_bf16, w_in, b_gates, norm_g, w_out, layer, batch, seq, heads=MLSTM_HEADS,
                 dk=MLSTM_QK_DIM, dv=MLSTM_V_DIM, chunk=MLSTM_KERNEL_CHUNK):
    n_main = 2 * heads * (dk + dv)
    proj = _matmul(x_bf16, w_in, layer, BF16, n_cols=n_main)
    w_gates_pad = jnp.zeros((1, w_in.shape[1], LANES), F32).at[0, :, :2 * heads].set(w_in[layer, :, n_main:])
    gates = _matmul(x_bf16, w_gates_pad, 0, F32)[:, :2 * heads]
    hg = _mlstm_recurrence(proj, gates, b_gates, norm_g, batch, seq, heads, dk, dv, min(chunk, seq))
    return _matmul(hg, w_out, layer, BF16)


def kernel(x, moba_w_in, moba_w_out, mlstm_w_in, mlstm_b_gates, mlstm_norm_g, mlstm_w_out, ln_g, ln_b,
           router_w, router_b, moe_w_gate_up, moe_b_gate_up, moe_w_down, moe_b_down):
    batch, seq, d = x.shape
    x_f32 = x.reshape(batch * seq, d)
    x_bf16 = x_f32.astype(BF16)
    for i in range(DEPTH):
        slot = i // 2
        if i % 2 == 0:
            y = _moba_mixer(x_bf16, moba_w_in, moba_w_out, slot, batch, seq)
        else:
            y = _mlstm_mixer(x_bf16, mlstm_w_in, mlstm_b_gates[slot], mlstm_norm_g[slot], mlstm_w_out, slot,
                             batch, seq)
        x_f32, x_bf16 = _post_mixer(x_f32, y, router_w[i], router_b[i], moe_w_gate_up, moe_b_gate_up,
                                    moe_w_down, moe_b_down, i, ln_g[i], ln_b[i])
    return x_f32.reshape(batch, seq, d)
```
